```python
import math
import jax, jax.numpy as jnp
from jax import lax
import numpy as np

D_MODEL = 2048
BATCH = 4
SEQ = 2048
DEPTH = 4
DEC_BATCH = 128
DEC_SEQ = 8
PAST_LEN = 16384
PAGE_SIZE = 128

S5_WIDTH = D_MODEL // 4
S5_GROUP_CH = 16
S5_GROUPS = S5_WIDTH // S5_GROUP_CH
S5_STATE = 64
RET_WIDTH = D_MODEL // 4
RET_HEADS = 4
RET_HEAD_DIM = RET_WIDTH // RET_HEADS
ROPE_BASE = 10000.0
SSD_WIDTH = D_MODEL // 2
SSD_HEAD_DIM = 64
SSD_HEADS = SSD_WIDTH // SSD_HEAD_DIM
SSD_GROUPS = 2
SSD_STATE = 128
SSD_CONV = 4
SSD_CONV_DIM = SSD_WIDTH + 2 * SSD_GROUPS * SSD_STATE
MIX_WIDTH = S5_WIDTH + RET_WIDTH + SSD_WIDTH
IN_COLS = S5_WIDTH + 4 * RET_WIDTH + SSD_WIDTH + SSD_CONV_DIM + SSD_HEADS
FFN_DIM = ((8 * D_MODEL // 3 + 255) // 256) * 256
MEM_TOKENS = 256
XATTN_HEADS = 4
XATTN_HEAD_DIM = D_MODEL // XATTN_HEADS
CHUNK = 128
EPS = 1e-6

kernel_name = 'hymba_s5_retention_ssd_macaron_step'


def rmsnorm(x, w):
    xf = x.astype(jnp.float32)
    y = xf * lax.rsqrt(jnp.mean(xf * xf, axis=-1, keepdims=True) + EPS)
    return (y * w.astype(jnp.float32)).astype(x.dtype)


def swiglu(h, w1, w3, w2):
    return (jax.nn.silu(h @ w1) * (h @ w3)) @ w2


def rope(t, pos):
    half = t.shape[-1] // 2
    inv = ROPE_BASE ** (-jnp.arange(half, dtype=jnp.float32) / half)
    ang = pos.astype(jnp.float32)[:, None] * inv[None, :]
    cos = jnp.cos(ang)[None, :, None, :]
    sin = jnp.sin(ang)[None, :, None, :]
    tf = t.astype(jnp.float32)
    t1, t2 = tf[..., :half], tf[..., half:]
    return jnp.concatenate([t1 * cos - t2 * sin, t1 * sin + t2 * cos], axis=-1)


def chunked_decay_scan(q, k, v, log_a, h0):
    f32 = jnp.float32
    b, l, nh, _ = q.shape
    p = v.shape[-1]
    chunk = math.gcd(l, CHUNK)
    nc = l // chunk

    def split(t):
        return t.astype(f32).reshape((b, nc, chunk) + t.shape[2:]).swapaxes(0, 1)

    qs, ks, vs, las = split(q), split(k), split(v), split(log_a)
    idx = jnp.arange(chunk)
    causal = (idx[:, None] >= idx[None, :])[None, :, :, None]

    def step(h_prev, inp):
        qc, kc, vc, lac = inp
        cum = jnp.cumsum(lac, axis=1)
        diff = cum[:, :, None, :] - cum[:, None, :, :]
        decay = jnp.exp(jnp.where(causal, diff, -jnp.inf))
        scores = jnp.einsum('bihn,bjhn->bijh', qc, kc) * decay
        y = jnp.einsum('bijh,bjhp->bihp', scores, vc)
        y = y + jnp.einsum('bihn,bhnp->bihp', qc * jnp.exp(cum)[..., None], h_prev)
        tail = jnp.exp(cum[:, -1:, :] - cum)
        h_new = (jnp.exp(cum[:, -1, :])[:, :, None, None] * h_prev
                 + jnp.einsum('bjhn,bjhp->bhnp', kc * tail[..., None], vc))
        return h_new, y

    h_fin, ys = lax.scan(step, h0.astype(f32), (qs, ks, vs, las))
    y = ys.swapaxes(0, 1).reshape(b, l, nh, p)
    return y, h_fin


def causal_dwconv(u, buf, w, bias):
    xpad = jnp.concatenate([buf.astype(u.dtype), u], axis=1)
    out = lax.conv_general_dilated(xpad, w[:, None, :].astype(u.dtype), window_strides=(1,), padding='VALID',
                                   dimension_numbers=('NWC', 'WIO', 'NWC'), feature_group_count=u.shape[-1])
    return out + bias.astype(u.dtype), xpad[:, -(SSD_CONV - 1):]


def _s5_combine(e1, e2):
    a1, b1 = e1
    a2, b2 = e2
    return a2 * a1, a2 * b1 + b2


def s5_mixer(u, h0_re, h0_im, lam_re, lam_im, b_re, b_im, c_re, c_im, d, log_step, glu_w, glu_b):
    f32 = jnp.float32
    bsz, L, _ = u.shape
    uf = u.astype(f32).reshape(bsz, L, S5_GROUPS, S5_GROUP_CH)
    lam = lax.complex(lam_re.astype(f32), lam_im.astype(f32))
    step = jnp.exp(log_step.astype(f32))[:, None]
    lam_bar = jnp.exp(lam * step)
    b_bar = ((lam_bar - 1.0) / lam)[..., None] * lax.complex(b_re.astype(f32), b_im.astype(f32))
    c_mat = lax.complex(c_re.astype(f32), c_im.astype(f32))
    bu = jnp.einsum('blgc,gpc->blgp', uf.astype(jnp.complex64), b_bar)
    h0 = lax.complex(h0_re.astype(f32), h0_im.astype(f32))
    bu = bu.at[:, 0].add(lam_bar * h0)
    a = jnp.broadcast_to(lam_bar, bu.shape)
    _, hs = lax.associative_scan(_s5_combine, (a, bu), axis=1)
    y = jnp.einsum('blgp,gcp->blgc', hs, c_mat).real + d.astype(f32).reshape(S5_GROUPS, S5_GROUP_CH) * uf
    y = jax.nn.gelu(y.reshape(bsz, L, S5_WIDTH))
    y = y * jax.nn.sigmoid(y @ glu_w.astype(f32) + glu_b.astype(f32))
    h_last = hs[:, -1]
    return y.astype(u.dtype), jnp.real(h_last), jnp.imag(h_last)


def trunk_layer(x, pos, mem_k, mem_v, s5_re0, s5_im0, ret0, ssm0, conv0, p):
    f32 = jnp.float32
    b, L, _ = x.shape
    x = x + 0.5 * swiglu(rmsnorm(x, p['ffn1_norm']), p['ffn1_w1'], p['ffn1_w3'], p['ffn1_w2'])

    proj = rmsnorm(x, p['mix_norm']) @ p['w_in']
    cuts = np.cumsum([S5_WIDTH, RET_WIDTH, RET_WIDTH, RET_WIDTH, RET_WIDTH, SSD_WIDTH, SSD_CONV_DIM]).tolist()
    u_s5, q_r, k_r, v_r, g_r, z_r, xbc_r, dt_r = jnp.split(proj, cuts, axis=-1)

    s5_out, s5_re, s5_im = s5_mixer(u_s5, s5_re0, s5_im0, p['s5_lambda_re'], p['s5_lambda_im'], p['s5_b_re'],
                                    p['s5_b_im'], p['s5_c_re'], p['s5_c_im'], p['s5_d'], p['s5_log_step'],
                                    p['s5_glu_w'], p['s5_glu_b'])

    hs = (b, L, RET_HEADS, RET_HEAD_DIM)
    q = rope(q_r.reshape(hs), pos)
    k = rope(k_r.reshape(hs), pos) * (RET_HEAD_DIM ** -0.5)
    log_gamma = jnp.log(1.0 - 2.0 ** (-5.0 - jnp.arange(RET_HEADS, dtype=f32)))
    log_a = jnp.broadcast_to(log_gamma, (b, L, RET_HEADS))
    o_r, ret_new = chunked_decay_scan(q, k, v_r.reshape(hs), log_a, ret0)
    o_r = o_r * lax.rsqrt(jnp.mean(o_r * o_r, axis=-1, keepdims=True) + EPS)
    o_r = o_r * p['ret_norm'].astype(f32).reshape(RET_HEADS, RET_HEAD_DIM)
    ret_out = (jax.nn.silu(g_r.astype(f32)) * o_r.reshape(b, L, RET_WIDTH)).astype(x.dtype)

    xbc, conv_new = causal_dwconv(xbc_r, conv0, p['ssd_conv_w'], p['ssd_conv_b'])
    xbc = jax.nn.silu(xbc.astype(f32))
    rep = SSD_HEADS // SSD_GROUPS
    xh = xbc[..., :SSD_WIDTH].reshape(b, L, SSD_HEADS, SSD_HEAD_DIM)
    bmat = jnp.repeat(xbc[..., SSD_WIDTH:SSD_WIDTH + SSD_GROUPS * SSD_STATE].reshape(b, L, SSD_GROUPS, SSD_STATE), rep, axis=2)
    cmat = jnp.repeat(xbc[..., SSD_WIDTH + SSD_GROUPS * SSD_STATE:].reshape(b, L, SSD_GROUPS, SSD_STATE), rep, axis=2)
    dt = jax.nn.softplus(dt_r.astype(f32) + p['ssd_dt_bias'].astype(f32))
    a_neg = -jnp.exp(p['ssd_a_log'].astype(f32))
    y, ssm_new = chunked_decay_scan(cmat, bmat, xh * dt[..., None], dt * a_neg, ssm0)
    y = y + xh * p['ssd_d'].astype(f32)[:, None]
    y = y.reshape(b, L, SSD_WIDTH) * jax.nn.silu(z_r.astype(f32))
    ssd_out = rmsnorm(y, p['ssd_norm']).astype(x.dtype)

    x = x + jnp.concatenate([s5_out, ret_out, ssd_out], axis=-1) @ p['w_out']

    qx = (rmsnorm(x, p['xattn_norm']) @ p['xattn_wq']).reshape(b, L, XATTN_HEADS, XATTN_HEAD_DIM)
    s = jnp.einsum('blhd,bmhd->bhlm', qx, mem_k.astype(qx.dtype)).astype(f32) * (XATTN_HEAD_DIM ** -0.5)
    pr = jax.nn.softmax(s, axis=-1).astype(x.dtype)
    att = jnp.einsum('bhlm,bmhd->blhd', pr, mem_v.astype(x.dtype)).reshape(b, L, D_MODEL)
    x = x + att @ p['xattn_wo']

    x = x + 0.5 * swiglu(rmsnorm(x, p['ffn2_norm']), p['ffn2_w1'], p['ffn2_w3'], p['ffn2_w2'])
    return x, s5_re, s5_im, ret_new, ssm_new, conv_new


def setup_inputs(seed: int = 0) -> dict:
    key = jax.random.key(seed)
    ks = iter(jax.random.split(key, 64))
    f32 = jnp.float32

    def nrm(shape, scale):
        return jax.random.normal(next(ks), shape, f32) * scale

    def gain(shape):
        return 1.0 + nrm(shape, 0.01)

    L = DEPTH
    d = D_MODEL
    inp = {}
    inp['x_prompt'] = nrm((BATCH, SEQ, d), 1.0)
    inp['x_sample'] = nrm((DEC_BATCH, DEC_SEQ, d), 1.0)
    inp['mem_prompt'] = nrm((BATCH, MEM_TOKENS, d), 1.0)
    inp['state_s5_re'] = nrm((L, DEC_BATCH, S5_GROUPS, S5_STATE), 0.3)
    inp['state_s5_im'] = nrm((L, DEC_BATCH, S5_GROUPS, S5_STATE), 0.3)
    inp['state_ret'] = nrm((L, DEC_BATCH, RET_HEADS, RET_HEAD_DIM, RET_HEAD_DIM), 1.0)
    inp['state_ssm'] = nrm((L, DEC_BATCH, SSD_HEADS, SSD_STATE, SSD_HEAD_DIM), 0.1)
    inp['state_conv'] = nrm((L, DEC_BATCH, SSD_CONV - 1, SSD_CONV_DIM), 1.0)
    inp['cache_mem_k'] = nrm((L, DEC_BATCH, MEM_TOKENS, XATTN_HEADS, XATTN_HEAD_DIM), 1.0)
    inp['cache_mem_v'] = nrm((L, DEC_BATCH, MEM_TOKENS, XATTN_HEADS, XATTN_HEAD_DIM), 1.0)
    inp['ffn1_norm'] = gain((L, d))
    inp['ffn1_w1'] = nrm((L, d, FFN_DIM), d ** -0.5)
    inp['ffn1_w3'] = nrm((L, d, FFN_DIM), d ** -0.5)
    inp['ffn1_w2'] = nrm((L, FFN_DIM, d), FFN_DIM ** -0.5)
    inp['mix_norm'] = gain((L, d))
    inp['w_in'] = nrm((L, d, IN_COLS), d ** -0.5)
    inp['w_out'] = nrm((L, MIX_WIDTH, d), MIX_WIDTH ** -0.5)
    inp['s5_lambda_re'] = -0.5 + nrm((L, S5_GROUPS, S5_STATE), 0.01)
    inp['s5_lambda_im'] = jnp.pi * jnp.arange(S5_STATE, dtype=f32) + nrm((L, S5_GROUPS, S5_STATE), 0.01)
    inp['s5_b_re'] = nrm((L, S5_GROUPS, S5_STATE, S5_GROUP_CH), (2 * S5_GROUP_CH) ** -0.5)
    inp['s5_b_im'] = nrm((L, S5_GROUPS, S5_STATE, S5_GROUP_CH), (2 * S5_GROUP_CH) ** -0.5)
    inp['s5_c_re'] = nrm((L, S5_GROUPS, S5_GROUP_CH, S5_STATE), S5_STATE ** -0.5)
    inp['s5_c_im'] = nrm((L, S5_GROUPS, S5_GROUP_CH, S5_STATE), S5_STATE ** -0.5)
    inp['s5_d'] = nrm((L, S5_WIDTH), 1.0)
    inp['s5_log_step'] = jax.random.uniform(next(ks), (L, S5_GROUPS), f32, math.log(1e-3), math.log(1e-1))
    inp['s5_glu_w'] = nrm((L, S5_WIDTH, S5_WIDTH), S5_WIDTH ** -0.5)
    inp['s5_glu_b'] = nrm((L, S5_WIDTH), 0.01)
    inp['ret_norm'] = gain((L, RET_WIDTH))
    inp['ssd_conv_w'] = nrm((L, SSD_CONV, SSD_CONV_DIM), SSD_CONV ** -0.5)
    inp['ssd_conv_b'] = nrm((L, SSD_CONV_DIM), 0.01)
    dt0 = jnp.exp(jax.random.uniform(next(ks), (L, SSD_HEADS), f32, math.log(1e-3), math.log(1e-1)))
    inp['ssd_dt_bias'] = dt0 + jnp.log(-jnp.expm1(-dt0))
    inp['ssd_a_log'] = jnp.log(jax.random.uniform(next(ks), (L, SSD_HEADS), f32, 1.0, 16.0))
    inp['ssd_d'] = gain((L, SSD_HEADS))
    inp['ssd_norm'] = gain((L, SSD_WIDTH))
    inp['xattn_norm'] = gain((L, d))
    inp['xattn_wq'] = nrm((L, d, d), d ** -0.5)
    inp['xattn_wk'] = nrm((L, d, d), d ** -0.5)
    inp['xattn_wv'] = nrm((L, d, d), d ** -0.5)
    inp['xattn_wo'] = nrm((L, d, d), d ** -0.5)
    inp['ffn2_norm'] = gain((L, d))
    inp['ffn2_w1'] = nrm((L, d, FFN_DIM), d ** -0.5)
    inp['ffn2_w3'] = nrm((L, d, FFN_DIM), d ** -0.5)
    inp['ffn2_w2'] = nrm((L, FFN_DIM, d), FFN_DIM ** -0.5)
    inp['final_norm'] = gain((d,))
    return inp


def reference(x_prompt, x_sample, mem_prompt, state_s5_re, state_s5_im, state_ret, state_ssm, state_conv,
              cache_mem_k, cache_mem_v, ffn1_norm, ffn1_w1, ffn1_w3, ffn1_w2, mix_norm, w_in, w_out,
              s5_lambda_re, s5_lambda_im, s5_b_re, s5_b_im, s5_c_re, s5_c_im, s5_d, s5_log_step, s5_glu_w,
              s5_glu_b, ret_norm, ssd_conv_w, ssd_conv_b, ssd_dt_bias, ssd_a_log, ssd_d, ssd_norm, xattn_norm,
              xattn_wq, xattn_wk, xattn_wv, xattn_wo, ffn2_norm, ffn2_w1, ffn2_w3, ffn2_w2, final_norm):
    f32 = jnp.float32
    bp, bs = x_prompt.shape[0], x_sample.shape[0]
    pos_p = jnp.arange(x_prompt.shape[1], dtype=jnp.int32)
    pos_s = PAST_LEN + jnp.arange(x_sample.shape[1], dtype=jnp.int32)
    xp, xs = x_prompt, x_sample
    p_s5re, p_s5im, p_ret, p_ssm, p_conv, p_mk, p_mv = [], [], [], [], [], [], []
    s_s5re, s_s5im, s_ret, s_ssm, s_conv = [], [], [], [], []
    for l in range(DEPTH):
        prm = dict(ffn1_norm=ffn1_norm[l], ffn1_w1=ffn1_w1[l], ffn1_w3=ffn1_w3[l], ffn1_w2=ffn1_w2[l],
                   mix_norm=mix_norm[l], w_in=w_in[l], w_out=w_out[l],
                   s5_lambda_re=s5_lambda_re[l], s5_lambda_im=s5_lambda_im[l], s5_b_re=s5_b_re[l],
                   s5_b_im=s5_b_im[l], s5_c_re=s5_c_re[l], s5_c_im=s5_c_im[l], s5_d=s5_d[l],
                   s5_log_step=s5_log_step[l], s5_glu_w=s5_glu_w[l], s5_glu_b=s5_glu_b[l],
                   ret_norm=ret_norm[l], ssd_conv_w=ssd_conv_w[l], ssd_conv_b=ssd_conv_b[l],
                   ssd_dt_bias=ssd_dt_bias[l], ssd_a_log=ssd_a_log[l], ssd_d=ssd_d[l], ssd_norm=ssd_norm[l],
                   xattn_norm=xattn_norm[l], xattn_wq=xattn_wq[l], xattn_wo=xattn_wo[l],
                   ffn2_norm=ffn2_norm[l], ffn2_w1=ffn2_w1[l], ffn2_w3=ffn2_w3[l], ffn2_w2=ffn2_w2[l])
        mk = (mem_prompt @ xattn_wk[l]).reshape(bp, -1, XATTN_HEADS, XATTN_HEAD_DIM)
        mv = (mem_prompt @ xattn_wv[l]).reshape(bp, -1, XATTN_HEADS, XATTN_HEAD_DIM)
        xp, a1, a2, a3, a4, a5 = trunk_layer(
            xp, pos_p, mk, mv,
            jnp.zeros((bp, S5_GROUPS, S5_STATE), f32), jnp.zeros((bp, S5_GROUPS, S5_STATE), f32),
            jnp.zeros((bp, RET_HEADS, RET_HEAD_DIM, RET_HEAD_DIM), f32),
            jnp.zeros((bp, SSD_HEADS, SSD_STATE, SSD_HEAD_DIM), f32),
            jnp.zeros((bp, SSD_CONV - 1, SSD_CONV_DIM), x_prompt.dtype), prm)
        p_s5re.append(a1); p_s5im.append(a2); p_ret.append(a3); p_ssm.append(a4); p_conv.append(a5)
        p_mk.append(mk); p_mv.append(mv)
        xs, b1, b2, b3, b4, b5 = trunk_layer(
            xs, pos_s, cache_mem_k[l], cache_mem_v[l], state_s5_re[l], state_s5_im[l], state_ret[l],
            state_ssm[l], state_conv[l], prm)
        s_s5re.append(b1); s_s5im.append(b2); s_ret.append(b3); s_ssm.append(b4); s_conv.append(b5)
    y_prompt = rmsnorm(xp, final_norm)
    y_sample = rmsnorm(xs, final_norm)
    return (y_prompt, y_sample,
            jnp.stack(p_s5re), jnp.stack(p_s5im), jnp.stack(p_ret), jnp.stack(p_ssm), jnp.stack(p_conv),
            jnp.stack(p_mk), jnp.stack(p_mv),
            jnp.stack(s_s5re), jnp.stack(s_s5im), jnp.stack(s_ret), jnp.stack(s_ssm), jnp.stack(s_conv))
```

```python
import functools
import math

import jax
import jax.numpy as jnp
from jax import lax
from jax.experimental import pallas as pl
from jax.experimental.pallas import tpu as pltpu

F32 = jnp.float32
BF16 = jnp.bfloat16

EPS = 1e-6
S5_GROUPS = 32
S5_GROUP_CH = 16
S5_STATE = 64
S5_MODES = S5_GROUPS * S5_STATE
RET_HEADS = 4
RET_HEAD_DIM = 128
ROPE_BASE = 10000.0
SSD_HEADS = 16
SSD_HEAD_DIM = 64
SSD_GROUPS = 2
SSD_STATE = 128
SSD_CONV = 4
MEM_TOKENS = 256
XATTN_HEADS = 4
CHUNK = 128
PAST_LEN = 16384

LANES = 128
SUBLANES = 8
VMEM_LIMIT_BYTES = 56 * 1024 * 1024

_NT = (((1,), (1,)), ((), ()))
_TN = (((0,), (0,)), ((), ()))


def _params(n_axes):
    return pltpu.CompilerParams(dimension_semantics=("arbitrary",) * n_axes,
                                vmem_limit_bytes=VMEM_LIMIT_BYTES)


def _bdot(a, b):
    return jnp.dot(a.astype(BF16), b.astype(BF16), preferred_element_type=F32)


def _split3(x):
    x1 = x.astype(BF16)
    r1 = x - x1.astype(F32)
    x2 = r1.astype(BF16)
    x3 = (r1 - x2.astype(F32)).astype(BF16)
    return x1, x2, x3


def _rmsnorm_kernel(x_ref, g_ref, o_ref):
    x = x_ref[...]
    ms = jnp.mean(x * x, axis=-1, keepdims=True)
    o_ref[...] = ((x * lax.rsqrt(ms + EPS)) * g_ref[...]).astype(o_ref.dtype)


def rmsnorm(x, gains, layer, out_dtype, tm=512):
    m, d = x.shape
    return pl.pallas_call(
        _rmsnorm_kernel,
        grid=(m // tm,),
        in_specs=[pl.BlockSpec((tm, d), lambda i: (i, 0)),
                  pl.BlockSpec((None, 1, d), lambda i: (layer, 0, 0))],
        out_specs=pl.BlockSpec((tm, d), lambda i: (i, 0)),
        out_shape=jax.ShapeDtypeStruct((m, d), out_dtype),
        compiler_params=_params(1),
        name="rmsnorm",
    )(x, gains)


def _mm_kernel(*refs, n_w, has_res, scale):
    a_ref = refs[0]
    w_refs = refs[1:1 + n_w]
    res_ref = refs[1 + n_w] if has_res else None
    o_ref = refs[1 + n_w + int(has_res)]
    wb_refs = refs[2 + n_w + int(has_res):]

    @pl.when(pl.program_id(1) == 0)
    def _():
        for w_ref, wb_ref in zip(w_refs, wb_refs):
            wb_ref[...] = w_ref[...].astype(BF16)

    a = a_ref[...]
    acc = jnp.dot(a, wb_refs[0][...], preferred_element_type=F32)
    if n_w == 2:
        acc = jax.nn.silu(acc) * jnp.dot(a, wb_refs[1][...], preferred_element_type=F32)
    if has_res:
        acc = res_ref[...] + (acc if scale == 1.0 else scale * acc)
    o_ref[...] = acc.astype(o_ref.dtype)


def matmul(a, ws, layer, *, n_cols, tn, tm, out_dtype, res=None, scale=1.0):
    m, k = a.shape
    n_w = len(ws)
    in_specs = [pl.BlockSpec((tm, k), lambda j, i: (i, 0))]
    in_specs += [pl.BlockSpec((None, k, tn), lambda j, i: (layer, 0, j)) for _ in ws]
    args = [a, *ws]
    if res is not None:
        in_specs.append(pl.BlockSpec((tm, tn), lambda j, i: (i, j)))
        args.append(res)
    return pl.pallas_call(
        functools.partial(_mm_kernel, n_w=n_w, has_res=res is not None, scale=scale),
        grid=(n_cols // tn, m // tm),
        in_specs=in_specs,
        out_specs=pl.BlockSpec((tm, tn), lambda j, i: (i, j)),
        out_shape=jax.ShapeDtypeStruct((m, n_cols), out_dtype),
        scratch_shapes=[pltpu.VMEM((k, tn), BF16) for _ in ws],
        compiler_params=_params(2),
        name="matmul",
    )(*args)


S5_LANE_CHUNKS = S5_MODES // LANES
S5_POWERS = 8


def _s5_kernel(u_ref, h0r_ref, h0i_ref, pwr_ref, pwi_ref, bblk_ref, cblk_ref, d_ref, gw_ref, gb_ref,
               o_ref, hor_ref, hoi_ref, bur_ref, bui_ref, hs_ref, *, seg, rb, chain):
    nseg = rb // seg
    nsteps = seg.bit_length() - 1
    u = u_ref[...]
    bu = _bdot(u, bblk_ref[...])
    for c in range(S5_LANE_CHUNKS):
        bur_ref[c] = bu[:, c * LANES:(c + 1) * LANES]
        bui_ref[c] = bu[:, S5_MODES + c * LANES:S5_MODES + (c + 1) * LANES]

    if chain:
        @pl.when(pl.program_id(1) == 0)
        def _():
            hor_ref[...] = h0r_ref[...]
            hoi_ref[...] = h0i_ref[...]
        hin_r, hin_i = hor_ref, hoi_ref
    else:
        hin_r, hin_i = h0r_ref, h0i_ref

    if nseg == 1:
        first = pl.ds(0, 1)
        last = pl.ds(rb - 1, 1)
    else:
        first = pl.ds(0, nseg, stride=seg)
        last = pl.ds(seg - 1, nseg, stride=seg)
    step_in_seg = lax.broadcasted_iota(jnp.int32, (rb, LANES), 0) % seg

    def chunk(c, carry):
        pr = pwr_ref[c]
        pi = pwi_ref[c]
        hr = hin_r[c]
        hi = hin_i[c]
        bur_ref[c, first, :] = bur_ref[c, first, :] + (pr[0:1] * hr - pi[0:1] * hi)
        bui_ref[c, first, :] = bui_ref[c, first, :] + (pr[0:1] * hi + pi[0:1] * hr)
        xr = bur_ref[c]
        xi = bui_ref[c]
        for k in range(nsteps):
            sh = 1 << k
            keep = step_in_seg >= sh
            sr = pltpu.roll(xr, sh, 0)
            si = pltpu.roll(xi, sh, 0)
            ar = pr[k:k + 1]
            ai = pi[k:k + 1]
            xr, xi = (xr + jnp.where(keep, ar * sr - ai * si, 0.0),
                      xi + jnp.where(keep, ar * si + ai * sr, 0.0))
        bur_ref[c] = xr
        bui_ref[c] = xi
        hor_ref[c] = bur_ref[c, last, :]
        hoi_ref[c] = bui_ref[c, last, :]
        col = pl.multiple_of(c * LANES, LANES)
        hs_ref[:, pl.ds(col, LANES)] = xr.astype(BF16)
        hs_ref[:, pl.ds(S5_MODES + col, LANES)] = xi.astype(BF16)
        return carry

    lax.fori_loop(0, S5_LANE_CHUNKS, chunk, 0)

    y = jnp.dot(hs_ref[...], cblk_ref[...], preferred_element_type=F32) + d_ref[...] * u
    y = jax.nn.gelu(y)
    gate = jax.nn.sigmoid(_bdot(y, gw_ref[...]) + gb_ref[...])
    o_ref[...] = (y * gate).astype(o_ref.dtype)


def s5_mixer(proj, row0, n_seq, seq_len, h0r, h0i, tabs, layer):
    width = S5_GROUPS * S5_GROUP_CH
    chain = seq_len > SUBLANES
    if chain:
        rb = 256
        seg = rb
        grid = (n_seq, seq_len // rb)
        rows = lambda s, b: (row0 // rb + s * (seq_len // rb) + b, 0)
        orow = lambda s, b: (s * (seq_len // rb) + b, 0)
        h0r = h0r.reshape(S5_LANE_CHUNKS, n_seq, 1, LANES)
        h0i = h0i.reshape(S5_LANE_CHUNKS, n_seq, 1, LANES)
        st_spec = pl.BlockSpec((S5_LANE_CHUNKS, None, 1, LANES), lambda s, b: (0, s, 0, 0))
        st_shape = jax.ShapeDtypeStruct((S5_LANE_CHUNKS, n_seq, 1, LANES), F32)
    else:
        rb = 256
        seg = seq_len
        nseg = rb // seg
        grid = (n_seq // nseg, 1)
        rows = lambda s, b: (row0 // rb + s, 0)
        orow = lambda s, b: (s, 0)
        st_spec = pl.BlockSpec((S5_LANE_CHUNKS, nseg, LANES), lambda s, b: (0, s, 0))
        st_shape = jax.ShapeDtypeStruct((S5_LANE_CHUNKS, n_seq, LANES), F32)
    const3 = lambda s, b: (layer, 0, 0)
    const4 = lambda s, b: (layer, 0, 0, 0)
    out, hr, hi = pl.pallas_call(
        functools.partial(_s5_kernel, seg=seg, rb=rb, chain=chain),
        grid=grid,
        in_specs=[pl.BlockSpec((rb, width), rows), st_spec, st_spec,
                  pl.BlockSpec((None, S5_LANE_CHUNKS, S5_POWERS, LANES), const4),
                  pl.BlockSpec((None, S5_LANE_CHUNKS, S5_POWERS, LANES), const4),
                  pl.BlockSpec((None, width, 2 * S5_MODES), const3),
                  pl.BlockSpec((None, 2 * S5_MODES, width), const3),
                  pl.BlockSpec((None, 1, width), const3),
                  pl.BlockSpec((None, width, width), const3),
                  pl.BlockSpec((None, 1, width), const3)],
        out_specs=[pl.BlockSpec((rb, width), orow), st_spec, st_spec],
        out_shape=[jax.ShapeDtypeStruct((n_seq * seq_len, width), BF16), st_shape, st_shape],
        scratch_shapes=[pltpu.VMEM((S5_LANE_CHUNKS, rb, LANES), F32),
                        pltpu.VMEM((S5_LANE_CHUNKS, rb, LANES), F32),
                        pltpu.VMEM((rb, 2 * S5_MODES), BF16)],
        compiler_params=_params(2),
        name="s5_mixer",
    )(proj, h0r, h0i, tabs["pw_re"], tabs["pw_im"], tabs["bblk"], tabs["cblk"], tabs["d"], tabs["glu_w"],
      tabs["glu_b"])
    return out, hr.reshape(S5_LANE_CHUNKS, n_seq, LANES), hi.reshape(S5_LANE_CHUNKS, n_seq, LANES)


def s5_tables(lam_re, lam_im, b_re, b_im, c_re, c_im, d, log_step, glu_w, glu_b):
    depth = lam_re.shape[0]
    step = jnp.exp(log_step.astype(F32))[..., None]
    lr, li = lam_re.astype(F32), lam_im.astype(F32)
    mag = jnp.exp(lr * step)
    lbr, lbi = mag * jnp.cos(li * step), mag * jnp.sin(li * step)
    den = lr * lr + li * li
    fr = ((lbr - 1.0) * lr + lbi * li) / den
    fi = (lbi * lr - (lbr - 1.0) * li) / den
    bbr = fr[..., None] * b_re - fi[..., None] * b_im
    bbi = fr[..., None] * b_im + fi[..., None] * b_re
    eye = jnp.eye(S5_GROUPS, dtype=F32)
    width = S5_GROUPS * S5_GROUP_CH

    def blk_in(t):
        return jnp.einsum("dgpc,gh->dgchp", t, eye).reshape(depth, width, S5_MODES)

    def blk_out(t):
        return jnp.einsum("dgcp,gh->dgphc", t, eye).reshape(depth, S5_MODES, width)

    bblk = jnp.concatenate([blk_in(bbr), blk_in(bbi)], axis=-1).astype(BF16)
    cblk = jnp.concatenate([blk_out(c_re.astype(F32)), blk_out(-c_im.astype(F32))], axis=1).astype(BF16)
    pr, pi = lbr.reshape(depth, S5_MODES), lbi.reshape(depth, S5_MODES)
    prs, pis = [], []
    for _ in range(S5_POWERS):
        prs.append(pr)
        pis.append(pi)
        pr, pi = pr * pr - pi * pi, 2.0 * pr * pi

    def chunked(ts):
        t = jnp.stack(ts, axis=1).reshape(depth, S5_POWERS, S5_LANE_CHUNKS, LANES)
        return t.transpose(0, 2, 1, 3)

    return dict(pw_re=chunked(prs), pw_im=chunked(pis), bblk=bblk, cblk=cblk,
                d=d.astype(F32).reshape(depth, 1, width), glu_w=glu_w.astype(BF16),
                glu_b=glu_b.astype(F32).reshape(depth, 1, width))


def _to_chunk_major(h):
    n = h.shape[0]
    return h.reshape(n, S5_LANE_CHUNKS, LANES).transpose(1, 0, 2)


def _from_chunk_major(h):
    n = h.shape[1]
    return h.transpose(1, 0, 2).reshape(n, S5_GROUPS, S5_STATE)


_RET_LOG_GAMMA = tuple(math.log(1.0 - 2.0 ** (-5.0 - h)) for h in range(RET_HEADS))


def _ret_kernel(q_ref, k_ref, v_ref, g_ref, cos_ref, sin_ref, nw_ref, s0_ref, o_ref, s_ref, y_ref, *, t, nb):
    @pl.when(pl.program_id(1) == 0)
    def _():
        s_ref[...] = s0_ref[...]

    cos = cos_ref[...]
    sin = sin_ref[...]
    ii = lax.broadcasted_iota(jnp.int32, (t, t), 0)
    jj = lax.broadcasted_iota(jnp.int32, (t, t), 1)
    ti = lax.broadcasted_iota(jnp.int32, (t, 1), 0).astype(F32)
    half = RET_HEAD_DIM // 2
    for h in range(RET_HEADS):
        lg = _RET_LOG_GAMMA[h]
        cols = slice(h * RET_HEAD_DIM, (h + 1) * RET_HEAD_DIM)
        decay = jnp.exp(jnp.where(ii >= jj, (ii - jj).astype(F32) * lg, -jnp.inf))
        grow = jnp.exp((ti + 1.0) * lg)
        tail = jnp.exp((t - 1.0 - ti) * lg)
        for n in range(nb):
            rows = slice(n * t, (n + 1) * t)
            q = q_ref[rows, cols]
            k = k_ref[rows, cols]
            v = v_ref[rows, cols].astype(BF16)
            q = q * cos + pltpu.roll(q, half, 1) * sin
            k = (k * cos + pltpu.roll(k, half, 1) * sin) * (RET_HEAD_DIM ** -0.5)
            s_prev = s_ref[n, h]
            scores = lax.dot_general(q.astype(BF16), k.astype(BF16), _NT, preferred_element_type=F32) * decay
            y = _bdot(scores, v) + _bdot(q * grow, s_prev)
            s_ref[n, h] = math.exp(t * lg) * s_prev + lax.dot_general(
                (k * tail).astype(BF16), v, _TN, preferred_element_type=F32)
            y = y * lax.rsqrt(jnp.mean(y * y, axis=-1, keepdims=True) + EPS)
            y = y * nw_ref[:, cols]
            y_ref[rows, cols] = jax.nn.silu(g_ref[rows, cols]) * y
    o_ref[...] = y_ref[...].astype(o_ref.dtype)


def retention_mixer(proj, row0, n_seq, seq_len, s0, s0_layer, cos2, sin2, norm_w, layer):
    width = RET_HEADS * RET_HEAD_DIM
    t = math.gcd(seq_len, CHUNK)
    n_chunks = seq_len // t
    nb = 1 if n_chunks > 1 else 8
    rb = nb * t

    def col(cb):
        return pl.BlockSpec((rb, width), lambda s, c: (row0 // rb + s * n_chunks + c, cb))

    st_block = (nb, RET_HEADS, RET_HEAD_DIM, RET_HEAD_DIM)
    tab_spec = pl.BlockSpec((t, RET_HEAD_DIM), lambda s, c: (c, 0))
    return pl.pallas_call(
        functools.partial(_ret_kernel, t=t, nb=nb),
        grid=(n_seq // nb, n_chunks),
        in_specs=[col(1), col(2), col(3), col(4), tab_spec, tab_spec,
                  pl.BlockSpec((None, 1, width), lambda s, c: (layer, 0, 0)),
                  pl.BlockSpec((None,) + st_block, lambda s, c: (s0_layer, s, 0, 0, 0))],
        out_specs=[pl.BlockSpec((rb, width), lambda s, c: (s * n_chunks + c, 0)),
                   pl.BlockSpec(st_block, lambda s, c: (s, 0, 0, 0))],
        out_shape=[jax.ShapeDtypeStruct((n_seq * seq_len, width), BF16),
                   jax.ShapeDtypeStruct(s0.shape[1:], F32)],
        scratch_shapes=[pltpu.VMEM((rb, width), F32)],
        compiler_params=_params(2),
        name="retention_mixer",
    )(proj, proj, proj, proj, cos2, sin2, norm_w, s0)


def rope_tables(pos):
    half = RET_HEAD_DIM // 2
    inv = ROPE_BASE ** (-jnp.arange(half, dtype=F32) / half)
    ang = pos.astype(F32)[:, None] * inv[None, :]
    cos, sin = jnp.cos(ang), jnp.sin(ang)
    return jnp.concatenate([cos, cos], axis=-1), jnp.concatenate([-sin, sin], axis=-1)


SSD_WIDTH = SSD_HEADS * SSD_HEAD_DIM
SSD_BC = SSD_GROUPS * SSD_STATE
SSD_CONV_DIM = SSD_WIDTH + 2 * SSD_BC
_SSD_PREV = SUBLANES


def _ssd_kernel(z0_ref, z1_ref, x0_ref, x1_ref, x2_ref, dt_ref, cs_ref, cw_ref, cb_ref, dtb_ref, aneg_ref, dsk_ref,
                nw_ref, s0_ref, o_ref, s_ref, co_ref, ext_ref, y_ref, *, t, nb):
    keep = SSD_CONV - 1

    @pl.when(pl.program_id(1) == 0)
    def _():
        s_ref[...] = s0_ref[...]
        for n in range(nb):
            ext_ref[n, _SSD_PREV - keep:_SSD_PREV, :] = cs_ref[n]

    ii = lax.broadcasted_iota(jnp.int32, (t, t), 0)
    jj = lax.broadcasted_iota(jnp.int32, (t, t), 1)
    causal = ii >= jj
    tri = causal.astype(BF16)
    cw = cw_ref[...]
    rep = SSD_HEADS // SSD_GROUPS
    for n in range(nb):
        rows = slice(n * t, (n + 1) * t)
        third = SSD_CONV_DIM // 3
        ext_ref[n, _SSD_PREV:_SSD_PREV + t, 0:third] = x0_ref[rows, :]
        ext_ref[n, _SSD_PREV:_SSD_PREV + t, third:2 * third] = x1_ref[rows, :]
        ext_ref[n, _SSD_PREV:_SSD_PREV + t, 2 * third:] = x2_ref[rows, :]
        conv = cb_ref[...] + cw[keep:keep + 1] * ext_ref[n, _SSD_PREV:_SSD_PREV + t, :]
        for back in range(1, SSD_CONV):
            conv = conv + cw[keep - back:keep - back + 1] * ext_ref[n, pl.ds(_SSD_PREV - back, t), :]
        tail_rows = ext_ref[n, _SSD_PREV + t - keep:_SSD_PREV + t, :]
        co_ref[n] = tail_rows
        ext_ref[n, _SSD_PREV - keep:_SSD_PREV, :] = tail_rows
        xbc = jax.nn.silu(conv)

        dt = jax.nn.softplus(dt_ref[rows, :] + dtb_ref[...])
        la = dt * aneg_ref[...]
        p1, p2, p3 = _split3(la)
        cum = (jnp.dot(tri, p1, preferred_element_type=F32) + jnp.dot(tri, p2, preferred_element_type=F32)
               + jnp.dot(tri, p3, preferred_element_type=F32))
        cum_t = cum.T
        c_last = cum[t - 1:t, :]
        e_cum = jnp.exp(cum)
        e_tail = jnp.exp(c_last - cum)
        e_last = jnp.exp(c_last)
        for g in range(SSD_GROUPS):
            bg = xbc[:, SSD_WIDTH + g * SSD_STATE:SSD_WIDTH + (g + 1) * SSD_STATE]
            cg = xbc[:, SSD_WIDTH + SSD_BC + g * SSD_STATE:SSD_WIDTH + SSD_BC + (g + 1) * SSD_STATE]
            gram = lax.dot_general(cg.astype(BF16), bg.astype(BF16), _NT, preferred_element_type=F32)
            for hh in range(rep):
                h = g * rep + hh
                hc = slice(h * SSD_HEAD_DIM, (h + 1) * SSD_HEAD_DIM)
                xh = xbc[:, hc]
                decay = jnp.exp(jnp.where(causal, cum[:, h:h + 1] - cum_t[h:h + 1, :], -jnp.inf))
                xdt = (xh * dt[:, h:h + 1]).astype(BF16)
                s_prev = s_ref[n, h]
                y = _bdot(gram * decay, xdt) + _bdot(cg * e_cum[:, h:h + 1], s_prev)
                s_ref[n, h] = e_last[:, h:h + 1] * s_prev + lax.dot_general(
                    (bg * e_tail[:, h:h + 1]).astype(BF16), xdt, _TN, preferred_element_type=F32)
                y_ref[rows, hc] = y + xh * dsk_ref[:, h:h + 1]
        half = SSD_WIDTH // 2
        y_ref[rows, :half] = y_ref[rows, :half] * jax.nn.silu(z0_ref[rows, :])
        y_ref[rows, half:] = y_ref[rows, half:] * jax.nn.silu(z1_ref[rows, :])
    y = y_ref[...]
    y = y * lax.rsqrt(jnp.mean(y * y, axis=-1, keepdims=True) + EPS)
    o_ref[...] = (y * nw_ref[...]).astype(o_ref.dtype)


_COL_BLOCK = 512
_SSD_Z_COL = 2560
_SSD_XBC_COL = _SSD_Z_COL + SSD_WIDTH


def ssd_mixer(proj, dt_raw, row0, n_seq, seq_len, s0, conv0, st_layer, tabs, layer):
    t = math.gcd(seq_len, CHUNK)
    n_chunks = seq_len // t
    nb = 1 if n_chunks > 1 else 4
    rb = nb * t

    def col(cb):
        return pl.BlockSpec((rb, _COL_BLOCK), lambda s, c: (row0 // rb + s * n_chunks + c, cb))

    z_cb = _SSD_Z_COL // _COL_BLOCK
    x_cb = _SSD_XBC_COL // _COL_BLOCK
    const3 = lambda s, c: (layer, 0, 0)
    st_block = (nb, SSD_HEADS, SSD_STATE, SSD_HEAD_DIM)
    cs_block = (nb, SSD_CONV - 1, SSD_CONV_DIM)
    return pl.pallas_call(
        functools.partial(_ssd_kernel, t=t, nb=nb),
        grid=(n_seq // nb, n_chunks),
        in_specs=[col(z_cb), col(z_cb + 1), col(x_cb), col(x_cb + 1), col(x_cb + 2),
                  pl.BlockSpec((rb, LANES), lambda s, c: (row0 // rb + s * n_chunks + c, 0)),
                  pl.BlockSpec((None,) + cs_block, lambda s, c: (st_layer, s, 0, 0)),
                  pl.BlockSpec((None, SSD_CONV, SSD_CONV_DIM), const3),
                  pl.BlockSpec((None, 1, SSD_CONV_DIM), const3),
                  pl.BlockSpec((None, 1, LANES), const3),
                  pl.BlockSpec((None, 1, LANES), const3),
                  pl.BlockSpec((None, 1, LANES), const3),
                  pl.BlockSpec((None, 1, SSD_WIDTH), const3),
                  pl.BlockSpec((None,) + st_block, lambda s, c: (st_layer, s, 0, 0, 0))],
        out_specs=[pl.BlockSpec((rb, SSD_WIDTH), lambda s, c: (s * n_chunks + c, 0)),
                   pl.BlockSpec(st_block, lambda s, c: (s, 0, 0, 0)),
                   pl.BlockSpec(cs_block, lambda s, c: (s, 0, 0))],
        out_shape=[jax.ShapeDtypeStruct((n_seq * seq_len, SSD_WIDTH), BF16),
                   jax.ShapeDtypeStruct(s0.shape[1:], F32),
                   jax.ShapeDtypeStruct(conv0.shape[1:], F32)],
        scratch_shapes=[pltpu.VMEM((nb, _SSD_PREV + t, SSD_CONV_DIM), F32),
                        pltpu.VMEM((rb, SSD_WIDTH), F32)],
        compiler_params=_params(2),
        name="ssd_mixer",
    )(proj, proj, proj, proj, proj, dt_raw, conv0, tabs["conv_w"], tabs["conv_b"], tabs["dt_bias"],
      tabs["a_neg"], tabs["d"], tabs["norm"], s0)


def _xattn_kernel(q_ref, k_ref, v_ref, o_ref, *, tq, nb, head_dim):
    for h in range(XATTN_HEADS):
        cols = slice(h * head_dim, (h + 1) * head_dim)
        q = q_ref[:, cols]
        pieces = []
        for n in range(nb):
            k = k_ref[n, :, cols].astype(BF16)
            s = lax.dot_general(q, k, _NT, preferred_element_type=F32)[n * tq:(n + 1) * tq]
            s = s * (head_dim ** -0.5)
            s = s - jnp.max(s, axis=-1, keepdims=True)
            p = jnp.exp(s)
            p = p / jnp.sum(p, axis=-1, keepdims=True)
            pieces.append(_bdot(p, v_ref[n, :, cols]))
        att = pieces[0] if nb == 1 else jnp.concatenate(pieces, axis=0)
        o_ref[:, cols] = att.astype(o_ref.dtype)


def cross_attention(q, row0, n_seq, seq_len, mem_k, mem_v, layer=None):
    d = q.shape[-1]
    if seq_len > SUBLANES:
        tq, nb = 512, 1
    else:
        tq, nb = seq_len, 2
    n_q = seq_len // tq
    if layer is None:
        kv_spec = pl.BlockSpec((nb, MEM_TOKENS, d), lambda s, i: (s, 0, 0))
    else:
        kv_spec = pl.BlockSpec((None, nb, MEM_TOKENS, d), lambda s, i: (layer, s, 0, 0))
    rb = nb * tq
    return pl.pallas_call(
        functools.partial(_xattn_kernel, tq=tq, nb=nb, head_dim=d // XATTN_HEADS),
        grid=(n_seq // nb, n_q),
        in_specs=[pl.BlockSpec((rb, d), lambda s, i: (row0 // rb + s * n_q + i, 0)), kv_spec, kv_spec],
        out_specs=pl.BlockSpec((rb, d), lambda s, i: (s * n_q + i, 0)),
        out_shape=jax.ShapeDtypeStruct((n_seq * seq_len, d), BF16),
        compiler_params=_params(2),
        name="cross_attention",
    )(q, mem_k, mem_v)


def kernel(x_prompt, x_sample, mem_prompt, state_s5_re, state_s5_im, state_ret, state_ssm, state_conv, cache_mem_k, cache_mem_v, ffn1_norm, ffn1_w1, ffn1_w3, ffn1_w2, mix_norm, w_in, w_out, s5_lambda_re, s5_lambda_im, s5_b_re, s5_b_im, s5_c_re, s5_c_im, s5_d, s5_log_step, s5_glu_w, s5_glu_b, ret_norm, ssd_conv_w, ssd_conv_b, ssd_dt_bias, ssd_a_log, ssd_d, ssd_norm, xattn_norm, xattn_wq, xattn_wk, xattn_wv, xattn_wo, ffn2_norm, ffn2_w1, ffn2_w3, ffn2_w2, final_norm):
    bp, lp, d = x_prompt.shape
    bs, ls, _ = x_sample.shape
    depth = w_in.shape[0]
    ffn = ffn1_w1.shape[-1]
    n_p, n_s = bp * lp, bs * ls
    tm = 512

    def gain(g):
        return g.astype(F32).reshape(-1, 1, g.shape[-1])

    def lane_pad(v):
        return jnp.pad(v.astype(F32), ((0, 0), (0, LANES - v.shape[-1]))).reshape(depth, 1, LANES)

    s5_tabs = s5_tables(s5_lambda_re, s5_lambda_im, s5_b_re, s5_b_im, s5_c_re, s5_c_im, s5_d, s5_log_step,
                        s5_glu_w, s5_glu_b)
    main_cols = w_in.shape[-1] - SSD_HEADS
    w_dt = jnp.pad(w_in[:, :, main_cols:], ((0, 0), (0, 0), (0, LANES - SSD_HEADS)))
    ssd_tabs = dict(conv_w=ssd_conv_w.astype(F32), conv_b=gain(ssd_conv_b), dt_bias=lane_pad(ssd_dt_bias),
                    a_neg=lane_pad(-jnp.exp(ssd_a_log.astype(F32))), d=lane_pad(ssd_d), norm=gain(ssd_norm))
    ret_w = gain(ret_norm)
    cos_p, sin_p = rope_tables(jnp.arange(lp, dtype=jnp.int32))
    cos_s, sin_s = rope_tables(PAST_LEN + jnp.arange(ls, dtype=jnp.int32))
    mem = mem_prompt.reshape(bp * MEM_TOKENS, d).astype(BF16)
    cache_k = cache_mem_k.reshape(depth, bs, MEM_TOKENS, d)
    cache_v = cache_mem_v.reshape(depth, bs, MEM_TOKENS, d)
    norms = {name: gain(g) for name, g in dict(ffn1=ffn1_norm, mix=mix_norm, xattn=xattn_norm, ffn2=ffn2_norm,
                                                final=final_norm).items()}

    zeros_s5 = jnp.zeros((S5_LANE_CHUNKS, bp, LANES), F32)
    zeros_ret = jnp.zeros((1, bp) + state_ret.shape[2:], F32)
    zeros_ssm = jnp.zeros((1, bp) + state_ssm.shape[2:], F32)
    zeros_conv = jnp.zeros((1, bp) + state_conv.shape[2:], F32)
    s5_re_cm = jnp.stack([_to_chunk_major(state_s5_re[l]) for l in range(depth)])
    s5_im_cm = jnp.stack([_to_chunk_major(state_s5_im[l]) for l in range(depth)])

    x = jnp.concatenate([x_prompt.reshape(n_p, d), x_sample.reshape(n_s, d)], axis=0)
    outs = {name: [] for name in ("p_s5re", "p_s5im", "p_ret", "p_ssm", "p_conv", "p_mk", "p_mv",
                                  "s_s5re", "s_s5im", "s_ret", "s_ssm", "s_conv")}

    def ffn_half(x, norm, w1, w3, w2, layer):
        h = rmsnorm(x, norm, layer, BF16)
        a = matmul(h, [w1, w3], layer, n_cols=ffn, tn=512, tm=tm, out_dtype=BF16)
        return matmul(a, [w2], layer, n_cols=d, tn=256, tm=tm, out_dtype=F32, res=x, scale=0.5)

    for l in range(depth):
        x = ffn_half(x, norms["ffn1"], ffn1_w1, ffn1_w3, ffn1_w2, l)

        h = rmsnorm(x, norms["mix"], l, BF16)
        proj = matmul(h, [w_in], l, n_cols=main_cols, tn=512, tm=tm, out_dtype=F32)
        dt_raw = matmul(h, [w_dt], l, n_cols=LANES, tn=LANES, tm=tm, out_dtype=F32)

        s5_p, hr_p, hi_p = s5_mixer(proj, 0, bp, lp, zeros_s5, zeros_s5, s5_tabs, l)
        s5_s, hr_s, hi_s = s5_mixer(proj, n_p, bs, ls, s5_re_cm[l], s5_im_cm[l], s5_tabs, l)
        ret_p, rs_p = retention_mixer(proj, 0, bp, lp, zeros_ret, 0, cos_p, sin_p, ret_w, l)
        ret_s, rs_s = retention_mixer(proj, n_p, bs, ls, state_ret, l, cos_s, sin_s, ret_w, l)
        ssd_p, ss_p, cv_p = ssd_mixer(proj, dt_raw, 0, bp, lp, zeros_ssm, zeros_conv, 0, ssd_tabs, l)
        ssd_s, ss_s, cv_s = ssd_mixer(proj, dt_raw, n_p, bs, ls, state_ssm, state_conv, l, ssd_tabs, l)
        mix = jnp.concatenate([jnp.concatenate([s5_p, ret_p, ssd_p], axis=1),
                               jnp.concatenate([s5_s, ret_s, ssd_s], axis=1)], axis=0)
        x = matmul(mix, [w_out], l, n_cols=d, tn=512, tm=tm, out_dtype=F32, res=x, scale=1.0)

        h = rmsnorm(x, norms["xattn"], l, BF16)
        q = matmul(h, [xattn_wq], l, n_cols=d, tn=512, tm=tm, out_dtype=BF16)
        mk = matmul(mem, [xattn_wk], l, n_cols=d, tn=512, tm=tm, out_dtype=F32)
        mv = matmul(mem, [xattn_wv], l, n_cols=d, tn=512, tm=tm, out_dtype=F32)
        att_p = cross_attention(q, 0, bp, lp, mk.astype(BF16).reshape(bp, MEM_TOKENS, d),
                                mv.astype(BF16).reshape(bp, MEM_TOKENS, d))
        att_s = cross_attention(q, n_p, bs, ls, cache_k, cache_v, layer=l)
        att = jnp.concatenate([att_p, att_s], axis=0)
        x = matmul(att, [xattn_wo], l, n_cols=d, tn=512, tm=tm, out_dtype=F32, res=x, scale=1.0)

        x = ffn_half(x, norms["ffn2"], ffn2_w1, ffn2_w3, ffn2_w2, l)

        outs["p_s5re"].append(_from_chunk_major(hr_p))
        outs["p_s5im"].append(_from_chunk_major(hi_p))
        outs["p_ret"].append(rs_p)
        outs["p_ssm"].append(ss_p)
        outs["p_conv"].append(cv_p)
        outs["p_mk"].append(mk.reshape(bp, MEM_TOKENS, XATTN_HEADS, d // XATTN_HEADS))
        outs["p_mv"].append(mv.reshape(bp, MEM_TOKENS, XATTN_HEADS, d // XATTN_HEADS))
        outs["s_s5re"].append(_from_chunk_major(hr_s))
        outs["s_s5im"].append(_from_chunk_major(hi_s))
        outs["s_ret"].append(rs_s)
        outs["s_ssm"].append(ss_s)
        outs["s_conv"].append(cv_s)

    y = rmsnorm(x, norms["final"], 0, F32)
    st = {name: jnp.stack(v) for name, v in outs.items()}
    return (y[:n_p].reshape(bp, lp, d), y[n_p:].reshape(bs, ls, d),
            st["p_s5re"], st["p_s5im"], st["p_ret"], st["p_ssm"], st["p_conv"], st["p_mk"], st["p_mv"],
            st["s_s5re"], st["s_s5im"], st["s_ret"], st["s_ssm"], st["s_conv"])
```

```python
import functools
import math

import jax
import jax.numpy as jnp
from jax import lax
from jax.experimental import pallas as pl
from jax.experimental.pallas import tpu as pltpu

F32 = jnp.float32
BF16 = jnp.bfloat16

EPS = 1e-6
S5_GROUPS = 32
S5_GROUP_CH = 16
S5_STATE = 64
S5_MODES = S5_GROUPS * S5_STATE
RET_HEADS = 4
RET_HEAD_DIM = 128
ROPE_BASE = 10000.0
SSD_HEADS = 16
SSD_HEAD_DIM = 64
SSD_GROUPS = 2
SSD_STATE = 128
SSD_CONV = 4
MEM_TOKENS = 256
XATTN_HEADS = 4
CHUNK = 128
PAST_LEN = 16384

MIX_WIDTH = 2048

LANES = 128
SUBLANES = 8
MIB = 1024 * 1024
VMEM_LIMIT_BYTES = 56 * MIB
MATMUL_VMEM_BUDGET = VMEM_LIMIT_BYTES - 10 * MIB

_NT = (((1,), (1,)), ((), ()))
_TN = (((0,), (0,)), ((), ()))


def _params(n_axes):
    return pltpu.CompilerParams(dimension_semantics=("arbitrary",) * n_axes,
                                vmem_limit_bytes=VMEM_LIMIT_BYTES)


def _pallas(body, *, grid, in_specs, args, out_specs, out_shapes, carried=(), scratch=(), name):
    carried = list(carried) + [None] * (len(out_shapes) - len(carried))
    extra = [(i, c) for i, c in enumerate(carried) if c is not None]
    n_in = len(args)

    def with_carried(*refs):
        body(*refs[:n_in], *refs[n_in + len(extra):])

    return pl.pallas_call(
        with_carried,
        grid=grid,
        in_specs=list(in_specs) + [pl.BlockSpec(memory_space=pl.ANY)] * len(extra),
        out_specs=out_specs,
        out_shape=out_shapes,
        input_output_aliases={n_in + k: i for k, (i, _) in enumerate(extra)},
        scratch_shapes=list(scratch),
        compiler_params=_params(len(grid)),
        name=name,
    )(*args, *[c for _, c in extra])


def _bdot(a, b):
    return jnp.dot(a.astype(BF16), b.astype(BF16), preferred_element_type=F32)


def _split3(x):
    x1 = x.astype(BF16)
    r1 = x - x1.astype(F32)
    x2 = r1.astype(BF16)
    x3 = (r1 - x2.astype(F32)).astype(BF16)
    return x1, x2, x3


def _rmsnorm_kernel(x_ref, g_ref, o_ref):
    x = x_ref[...]
    ms = jnp.mean(x * x, axis=-1, keepdims=True)
    o_ref[...] = ((x * lax.rsqrt(ms + EPS)) * g_ref[...]).astype(o_ref.dtype)


def rmsnorm(x, gains, layer, out_dtype, tm=512):
    m, d = x.shape
    return pl.pallas_call(
        _rmsnorm_kernel,
        grid=(m // tm,),
        in_specs=[pl.BlockSpec((tm, d), lambda i: (i, 0)),
                  pl.BlockSpec((None, 1, d), lambda i: (layer, 0, 0))],
        out_specs=pl.BlockSpec((tm, d), lambda i: (i, 0)),
        out_shape=jax.ShapeDtypeStruct((m, d), out_dtype),
        compiler_params=_params(1),
        name="rmsnorm",
    )(x, gains)


def _mm_kernel(*refs, n_w, has_res, scale):
    a_ref = refs[0]
    w_refs = refs[1:1 + n_w]
    res_ref = refs[1 + n_w] if has_res else None
    o_ref = refs[1 + n_w + int(has_res)]
    wb_refs = refs[2 + n_w + int(has_res):]

    @pl.when(pl.program_id(1) == 0)
    def _():
        for w_ref, wb_ref in zip(w_refs, wb_refs):
            wb_ref[...] = w_ref[...].astype(BF16)

    a = a_ref[...]
    acc = jnp.dot(a, wb_refs[0][...], preferred_element_type=F32)
    if n_w == 2:
        acc = jax.nn.silu(acc) * jnp.dot(a, wb_refs[1][...], preferred_element_type=F32)
    if has_res:
        acc = res_ref[...] + (acc if scale == 1.0 else scale * acc)
    o_ref[...] = acc.astype(o_ref.dtype)


def _matmul_tiles(m, k, n_cols, n_w, has_res, out_bytes):
    for tn in (1024, 512, 256, 128):
        if n_cols % tn:
            continue
        for tm in (1024, 512, 256):
            if m % tm:
                continue
            need = (2 * tm * k * 2
                    + n_w * (2 * k * tn * 4 + k * tn * 2)
                    + 2 * tm * tn * out_bytes + (2 * tm * tn * 4 if has_res else 0)
                    + n_w * tm * tn * 4)
            if need <= MATMUL_VMEM_BUDGET:
                return tn, tm
    raise ValueError("no matmul tiling fits VMEM")


def matmul(a, ws, layer, *, n_cols, out_dtype, res=None, scale=1.0):
    m, k = a.shape
    n_w = len(ws)
    tn, tm = _matmul_tiles(m, k, n_cols, n_w, res is not None, jnp.dtype(out_dtype).itemsize)
    in_specs = [pl.BlockSpec((tm, k), lambda j, i: (i, 0))]
    in_specs += [pl.BlockSpec((None, k, tn), lambda j, i: (layer, 0, j)) for _ in ws]
    args = [a, *ws]
    if res is not None:
        in_specs.append(pl.BlockSpec((tm, tn), lambda j, i: (i, j)))
        args.append(res)
    return pl.pallas_call(
        functools.partial(_mm_kernel, n_w=n_w, has_res=res is not None, scale=scale),
        grid=(n_cols // tn, m // tm),
        in_specs=in_specs,
        out_specs=pl.BlockSpec((tm, tn), lambda j, i: (i, j)),
        out_shape=jax.ShapeDtypeStruct((m, n_cols), out_dtype),
        scratch_shapes=[pltpu.VMEM((k, tn), BF16) for _ in ws],
        compiler_params=_params(2),
        name="matmul",
    )(*args)


S5_LANE_CHUNKS = S5_MODES // LANES
S5_POWERS = 8
S5_WIDTH = S5_GROUPS * S5_GROUP_CH


def _s5_kernel(u_ref, h0r_ref, h0i_ref, pwr_ref, pwi_ref, bblk_ref, cblk_ref, d_ref, gw_ref, gb_ref,
               o_ref, hor_ref, hoi_ref, bur_ref, bui_ref, hs_ref, *, seg, rb, chain):
    nseg = rb // seg
    nsteps = seg.bit_length() - 1
    u = u_ref[...]
    bu = _bdot(u, bblk_ref[...])
    for c in range(S5_LANE_CHUNKS):
        bur_ref[c] = bu[:, c * LANES:(c + 1) * LANES]
        bui_ref[c] = bu[:, S5_MODES + c * LANES:S5_MODES + (c + 1) * LANES]

    if chain:
        @pl.when(pl.program_id(1) == 0)
        def _():
            hor_ref[...] = h0r_ref[...]
            hoi_ref[...] = h0i_ref[...]
        hin_r, hin_i = hor_ref, hoi_ref
    else:
        hin_r, hin_i = h0r_ref, h0i_ref

    if nseg == 1:
        first = pl.ds(0, 1)
        last = pl.ds(rb - 1, 1)
    else:
        first = pl.ds(0, nseg, stride=seg)
        last = pl.ds(seg - 1, nseg, stride=seg)
    step_in_seg = lax.broadcasted_iota(jnp.int32, (rb, LANES), 0) % seg

    def chunk(c, carry):
        pr = pwr_ref[c]
        pi = pwi_ref[c]
        hr = hin_r[c]
        hi = hin_i[c]
        bur_ref[c, first, :] = bur_ref[c, first, :] + (pr[0:1] * hr - pi[0:1] * hi)
        bui_ref[c, first, :] = bui_ref[c, first, :] + (pr[0:1] * hi + pi[0:1] * hr)
        xr = bur_ref[c]
        xi = bui_ref[c]
        for k in range(nsteps):
            sh = 1 << k
            keep = step_in_seg >= sh
            sr = pltpu.roll(xr, sh, 0)
            si = pltpu.roll(xi, sh, 0)
            ar = pr[k:k + 1]
            ai = pi[k:k + 1]
            xr, xi = (xr + jnp.where(keep, ar * sr - ai * si, 0.0),
                      xi + jnp.where(keep, ar * si + ai * sr, 0.0))
        bur_ref[c] = xr
        bui_ref[c] = xi
        hor_ref[c] = bur_ref[c, last, :]
        hoi_ref[c] = bui_ref[c, last, :]
        col = pl.multiple_of(c * LANES, LANES)
        hs_ref[:, pl.ds(col, LANES)] = xr.astype(BF16)
        hs_ref[:, pl.ds(S5_MODES + col, LANES)] = xi.astype(BF16)
        return carry

    lax.fori_loop(0, S5_LANE_CHUNKS, chunk, 0)

    y = jnp.dot(hs_ref[...], cblk_ref[...], preferred_element_type=F32) + d_ref[...] * u
    y = jax.nn.gelu(y)
    gate = jax.nn.sigmoid(_bdot(y, gw_ref[...]) + gb_ref[...])
    o_ref[...] = (y * gate).astype(o_ref.dtype)


def s5_mixer(proj, mix, row0, n_seq, seq_len, h0r, h0i, tabs, layer):
    chain = seq_len > SUBLANES
    rb = 256
    if chain:
        seg = rb
        grid = (n_seq, seq_len // rb)
        rows = lambda s, b: (row0 // rb + s * (seq_len // rb) + b, 0)
        h0r = h0r.reshape(S5_LANE_CHUNKS, n_seq, 1, LANES)
        h0i = h0i.reshape(S5_LANE_CHUNKS, n_seq, 1, LANES)
        st_spec = pl.BlockSpec((S5_LANE_CHUNKS, None, 1, LANES), lambda s, b: (0, s, 0, 0))
        st_shape = jax.ShapeDtypeStruct((S5_LANE_CHUNKS, n_seq, 1, LANES), F32)
    else:
        seg = seq_len
        nseg = rb // seg
        grid = (n_seq // nseg, 1)
        rows = lambda s, b: (row0 // rb + s, 0)
        st_spec = pl.BlockSpec((S5_LANE_CHUNKS, nseg, LANES), lambda s, b: (0, s, 0))
        st_shape = jax.ShapeDtypeStruct((S5_LANE_CHUNKS, n_seq, LANES), F32)
    const3 = lambda s, b: (layer, 0, 0)
    const4 = lambda s, b: (layer, 0, 0, 0)
    mix, hr, hi = _pallas(
        functools.partial(_s5_kernel, seg=seg, rb=rb, chain=chain),
        grid=grid,
        in_specs=[pl.BlockSpec((rb, S5_WIDTH), rows), st_spec, st_spec,
                  pl.BlockSpec((None, S5_LANE_CHUNKS, S5_POWERS, LANES), const4),
                  pl.BlockSpec((None, S5_LANE_CHUNKS, S5_POWERS, LANES), const4),
                  pl.BlockSpec((None, S5_WIDTH, 2 * S5_MODES), const3),
                  pl.BlockSpec((None, 2 * S5_MODES, S5_WIDTH), const3),
                  pl.BlockSpec((None, 1, S5_WIDTH), const3),
                  pl.BlockSpec((None, S5_WIDTH, S5_WIDTH), const3),
                  pl.BlockSpec((None, 1, S5_WIDTH), const3)],
        args=[proj, h0r, h0i, tabs["pw_re"], tabs["pw_im"], tabs["bblk"], tabs["cblk"], tabs["d"], tabs["glu_w"],
              tabs["glu_b"]],
        out_specs=[pl.BlockSpec((rb, S5_WIDTH), rows), st_spec, st_spec],
        out_shapes=[jax.ShapeDtypeStruct((proj.shape[0], MIX_WIDTH), BF16), st_shape, st_shape],
        carried=[mix],
        scratch=[pltpu.VMEM((S5_LANE_CHUNKS, rb, LANES), F32),
                 pltpu.VMEM((S5_LANE_CHUNKS, rb, LANES), F32),
                 pltpu.VMEM((rb, 2 * S5_MODES), BF16)],
        name="s5_mixer",
    )
    return mix, hr.reshape(S5_LANE_CHUNKS, n_seq, LANES), hi.reshape(S5_LANE_CHUNKS, n_seq, LANES)


def s5_tables(lam_re, lam_im, b_re, b_im, c_re, c_im, d, log_step, glu_w, glu_b):
    depth = lam_re.shape[0]
    step = jnp.exp(log_step.astype(F32))[..., None]
    lr, li = lam_re.astype(F32), lam_im.astype(F32)
    mag = jnp.exp(lr * step)
    lbr, lbi = mag * jnp.cos(li * step), mag * jnp.sin(li * step)
    den = lr * lr + li * li
    fr = ((lbr - 1.0) * lr + lbi * li) / den
    fi = (lbi * lr - (lbr - 1.0) * li) / den
    bbr = fr[..., None] * b_re - fi[..., None] * b_im
    bbi = fr[..., None] * b_im + fi[..., None] * b_re
    eye = jnp.eye(S5_GROUPS, dtype=F32)

    def blk_in(t):
        return jnp.einsum("dgpc,gh->dgchp", t, eye).reshape(depth, S5_WIDTH, S5_MODES)

    def blk_out(t):
        return jnp.einsum("dgcp,gh->dgphc", t, eye).reshape(depth, S5_MODES, S5_WIDTH)

    bblk = jnp.concatenate([blk_in(bbr), blk_in(bbi)], axis=-1).astype(BF16)
    cblk = jnp.concatenate([blk_out(c_re.astype(F32)), blk_out(-c_im.astype(F32))], axis=1).astype(BF16)
    pr, pi = lbr.reshape(depth, S5_MODES), lbi.reshape(depth, S5_MODES)
    prs, pis = [], []
    for _ in range(S5_POWERS):
        prs.append(pr)
        pis.append(pi)
        pr, pi = pr * pr - pi * pi, 2.0 * pr * pi

    def chunked(ts):
        t = jnp.stack(ts, axis=1).reshape(depth, S5_POWERS, S5_LANE_CHUNKS, LANES)
        return t.transpose(0, 2, 1, 3)

    return dict(pw_re=chunked(prs), pw_im=chunked(pis), bblk=bblk, cblk=cblk,
                d=d.astype(F32).reshape(depth, 1, S5_WIDTH), glu_w=glu_w.astype(BF16),
                glu_b=glu_b.astype(F32).reshape(depth, 1, S5_WIDTH))


def _to_chunk_major(h):
    n = h.shape[0]
    return h.reshape(n, S5_LANE_CHUNKS, LANES).transpose(1, 0, 2)


def _from_chunk_major(h):
    n = h.shape[1]
    return h.transpose(1, 0, 2).reshape(n, S5_GROUPS, S5_STATE)


_RET_LOG_GAMMA = tuple(math.log(1.0 - 2.0 ** (-5.0 - h)) for h in range(RET_HEADS))
RET_WIDTH = RET_HEADS * RET_HEAD_DIM


def _ret_kernel(q_ref, k_ref, v_ref, g_ref, cos_ref, sin_ref, nw_ref, s0_ref, o_ref, s_ref, y_ref, *, t, nb):
    @pl.when(pl.program_id(1) == 0)
    def _():
        s_ref[...] = s0_ref[...]

    cos = cos_ref[...]
    sin = sin_ref[...]
    ii = lax.broadcasted_iota(jnp.int32, (t, t), 0)
    jj = lax.broadcasted_iota(jnp.int32, (t, t), 1)
    ti = lax.broadcasted_iota(jnp.int32, (t, 1), 0).astype(F32)
    half = RET_HEAD_DIM // 2
    for h in range(RET_HEADS):
        lg = _RET_LOG_GAMMA[h]
        cols = slice(h * RET_HEAD_DIM, (h + 1) * RET_HEAD_DIM)
        decay = jnp.exp(jnp.where(ii >= jj, (ii - jj).astype(F32) * lg, -jnp.inf))
        grow = jnp.exp((ti + 1.0) * lg)
        tail = jnp.exp((t - 1.0 - ti) * lg)
        for n in range(nb):
            rows = slice(n * t, (n + 1) * t)
            q = q_ref[rows, cols]
            k = k_ref[rows, cols]
            v = v_ref[rows, cols].astype(BF16)
            q = q * cos + pltpu.roll(q, half, 1) * sin
            k = (k * cos + pltpu.roll(k, half, 1) * sin) * (RET_HEAD_DIM ** -0.5)
            s_prev = s_ref[n, h]
            scores = lax.dot_general(q.astype(BF16), k.astype(BF16), _NT, preferred_element_type=F32) * decay
            y = _bdot(scores, v) + _bdot(q * grow, s_prev)
            s_ref[n, h] = math.exp(t * lg) * s_prev + lax.dot_general(
                (k * tail).astype(BF16), v, _TN, preferred_element_type=F32)
            y = y * lax.rsqrt(jnp.mean(y * y, axis=-1, keepdims=True) + EPS)
            y = y * nw_ref[:, cols]
            y_ref[rows, cols] = jax.nn.silu(g_ref[rows, cols]) * y
    o_ref[...] = y_ref[...].astype(o_ref.dtype)


def retention_mixer(proj, mix, row0, n_seq, seq_len, s0, s0_layer, s_out, depth, cos2, sin2, norm_w, layer):
    t = math.gcd(seq_len, CHUNK)
    n_chunks = seq_len // t
    nb = 1 if n_chunks > 1 else 8
    rb = nb * t

    def col(cb):
        return pl.BlockSpec((rb, RET_WIDTH), lambda s, c: (row0 // rb + s * n_chunks + c, cb))

    st_block = (None, nb, RET_HEADS, RET_HEAD_DIM, RET_HEAD_DIM)
    tab_spec = pl.BlockSpec((t, RET_HEAD_DIM), lambda s, c: (c, 0))
    return _pallas(
        functools.partial(_ret_kernel, t=t, nb=nb),
        grid=(n_seq // nb, n_chunks),
        in_specs=[col(1), col(2), col(3), col(4), tab_spec, tab_spec,
                  pl.BlockSpec((None, 1, RET_WIDTH), lambda s, c: (layer, 0, 0)),
                  pl.BlockSpec(st_block, lambda s, c: (s0_layer, s, 0, 0, 0))],
        args=[proj, proj, proj, proj, cos2, sin2, norm_w, s0],
        out_specs=[col(1), pl.BlockSpec(st_block, lambda s, c: (layer, s, 0, 0, 0))],
        out_shapes=[jax.ShapeDtypeStruct((proj.shape[0], MIX_WIDTH), BF16),
                    jax.ShapeDtypeStruct((depth,) + s0.shape[1:], F32)],
        carried=[mix, s_out],
        scratch=[pltpu.VMEM((rb, RET_WIDTH), F32)],
        name="retention_mixer",
    )


def rope_tables(pos):
    half = RET_HEAD_DIM // 2
    inv = ROPE_BASE ** (-jnp.arange(half, dtype=F32) / half)
    ang = pos.astype(F32)[:, None] * inv[None, :]
    cos, sin = jnp.cos(ang), jnp.sin(ang)
    return jnp.concatenate([cos, cos], axis=-1), jnp.concatenate([-sin, sin], axis=-1)


SSD_WIDTH = SSD_HEADS * SSD_HEAD_DIM
SSD_BC = SSD_GROUPS * SSD_STATE
SSD_CONV_DIM = SSD_WIDTH + 2 * SSD_BC
_SSD_PREV = SUBLANES
_COL_BLOCK = 512
_SSD_Z_COL = 2560
_SSD_XBC_COL = _SSD_Z_COL + SSD_WIDTH


def _ssd_kernel(z0_ref, z1_ref, x0_ref, x1_ref, x2_ref, dt_ref, cs_ref, cw_ref, cb_ref, dtb_ref, aneg_ref, dsk_ref,
                nw_ref, s0_ref, o_ref, s_ref, co_ref, ext_ref, y_ref, xdt_ref, xtl_ref, *, t, nb, conv_rows):
    keep = SSD_CONV - 1
    rep = SSD_HEADS // SSD_GROUPS
    grp = rep * SSD_HEAD_DIM

    def conv_row(n):
        return pl.ds(n, 1) if conv_rows == nb else pl.ds(pl.program_id(0) * nb + n, 1)

    @pl.when(pl.program_id(1) == 0)
    def _():
        s_ref[...] = s0_ref[...]
        for n in range(nb):
            for r in range(keep):
                ext_ref[n, _SSD_PREV - keep + r:_SSD_PREV - keep + r + 1, :] = cs_ref[r, conv_row(n), :]

    ii = lax.broadcasted_iota(jnp.int32, (t, t), 0)
    jj = lax.broadcasted_iota(jnp.int32, (t, t), 1)
    causal = ii >= jj
    tri = causal.astype(BF16)
    cw = cw_ref[...]
    for n in range(nb):
        rows = slice(n * t, (n + 1) * t)
        ext_ref[n, _SSD_PREV:_SSD_PREV + t, 0:_COL_BLOCK] = x0_ref[rows, :]
        ext_ref[n, _SSD_PREV:_SSD_PREV + t, _COL_BLOCK:2 * _COL_BLOCK] = x1_ref[rows, :]
        ext_ref[n, _SSD_PREV:_SSD_PREV + t, 2 * _COL_BLOCK:] = x2_ref[rows, :]
        conv = cb_ref[...] + cw[keep:keep + 1] * ext_ref[n, _SSD_PREV:_SSD_PREV + t, :]
        for back in range(1, SSD_CONV):
            conv = conv + cw[keep - back:keep - back + 1] * ext_ref[n, pl.ds(_SSD_PREV - back, t), :]
        tail_rows = ext_ref[n, _SSD_PREV + t - keep:_SSD_PREV + t, :]
        for r in range(keep):
            co_ref[r, conv_row(n), :] = tail_rows[r:r + 1]
        ext_ref[n, _SSD_PREV - keep:_SSD_PREV, :] = tail_rows
        xbc = jax.nn.silu(conv)

        dt = jax.nn.softplus(dt_ref[rows, :] + dtb_ref[...])
        la = dt * aneg_ref[...]
        p1, p2, p3 = _split3(la)
        cum = (jnp.dot(tri, p1, preferred_element_type=F32) + jnp.dot(tri, p2, preferred_element_type=F32)
               + jnp.dot(tri, p3, preferred_element_type=F32))
        cum_t = cum.T
        c_last = cum[t - 1:t, :]
        e_cum = jnp.exp(cum)
        e_tail = jnp.exp(c_last - cum)
        e_last = jnp.exp(c_last)
        for h in range(SSD_HEADS):
            hc = slice(h * SSD_HEAD_DIM, (h + 1) * SSD_HEAD_DIM)
            xdt = xbc[:, hc] * dt[:, h:h + 1]
            xdt_ref[rows, hc] = xdt
            xtl_ref[rows, hc] = xdt * e_tail[:, h:h + 1]
        for g in range(SSD_GROUPS):
            bg = xbc[:, SSD_WIDTH + g * SSD_STATE:SSD_WIDTH + (g + 1) * SSD_STATE].astype(BF16)
            cg = xbc[:, SSD_WIDTH + SSD_BC + g * SSD_STATE:SSD_WIDTH + SSD_BC + (g + 1) * SSD_STATE].astype(BF16)
            gram = lax.dot_general(cg, bg, _NT, preferred_element_type=F32)
            s_grp = s_ref[n, g * rep:(g + 1) * rep].reshape(grp, SSD_STATE)
            from_state = lax.dot_general(cg, s_grp.astype(BF16), _NT, preferred_element_type=F32)
            s_add = lax.dot_general(xtl_ref[rows, g * grp:(g + 1) * grp].astype(BF16), bg, _TN,
                                    preferred_element_type=F32)
            for hh in range(rep):
                h = g * rep + hh
                hc = slice(h * SSD_HEAD_DIM, (h + 1) * SSD_HEAD_DIM)
                lc = slice(hh * SSD_HEAD_DIM, (hh + 1) * SSD_HEAD_DIM)
                decay = jnp.exp(jnp.where(causal, cum[:, h:h + 1] - cum_t[h:h + 1, :], -jnp.inf))
                y = _bdot(gram * decay, xdt_ref[rows, hc]) + e_cum[:, h:h + 1] * from_state[:, lc]
                s_ref[n, h] = e_last[:, h:h + 1] * s_ref[n, h] + s_add[lc, :]
                y_ref[rows, hc] = y + xbc[:, hc] * dsk_ref[:, h:h + 1]
        y_ref[rows, :_COL_BLOCK] = y_ref[rows, :_COL_BLOCK] * jax.nn.silu(z0_ref[rows, :])
        y_ref[rows, _COL_BLOCK:] = y_ref[rows, _COL_BLOCK:] * jax.nn.silu(z1_ref[rows, :])
    y = y_ref[...]
    y = y * lax.rsqrt(jnp.mean(y * y, axis=-1, keepdims=True) + EPS)
    o_ref[...] = (y * nw_ref[...]).astype(o_ref.dtype)


def ssd_mixer(proj, dt_raw, mix, row0, n_seq, seq_len, s0, conv0, st_layer, s_out, conv_out, depth, tabs, layer):
    t = math.gcd(seq_len, CHUNK)
    n_chunks = seq_len // t
    nb = 1 if n_chunks > 1 else SUBLANES
    rb = nb * t
    conv_rows = nb if nb % SUBLANES == 0 else n_seq

    def col(cb, width=_COL_BLOCK):
        return pl.BlockSpec((rb, width), lambda s, c: (row0 // rb + s * n_chunks + c, cb))

    def conv_idx(lyr):
        return lambda s, c: (lyr, 0, s if conv_rows == nb else 0, 0)

    z_cb = _SSD_Z_COL // _COL_BLOCK
    x_cb = _SSD_XBC_COL // _COL_BLOCK
    const3 = lambda s, c: (layer, 0, 0)
    st_block = (None, nb, SSD_HEADS, SSD_HEAD_DIM, SSD_STATE)
    cs_block = (None, SSD_CONV - 1, conv_rows, SSD_CONV_DIM)
    return _pallas(
        functools.partial(_ssd_kernel, t=t, nb=nb, conv_rows=conv_rows),
        grid=(n_seq // nb, n_chunks),
        in_specs=[col(z_cb), col(z_cb + 1), col(x_cb), col(x_cb + 1), col(x_cb + 2), col(0, LANES),
                  pl.BlockSpec(cs_block, conv_idx(st_layer)),
                  pl.BlockSpec((None, SSD_CONV, SSD_CONV_DIM), const3),
                  pl.BlockSpec((None, 1, SSD_CONV_DIM), const3),
                  pl.BlockSpec((None, 1, LANES), const3),
                  pl.BlockSpec((None, 1, LANES), const3),
                  pl.BlockSpec((None, 1, LANES), const3),
                  pl.BlockSpec((None, 1, SSD_WIDTH), const3),
                  pl.BlockSpec(st_block, lambda s, c: (st_layer, s, 0, 0, 0))],
        args=[proj, proj, proj, proj, proj, dt_raw, conv0, tabs["conv_w"], tabs["conv_b"], tabs["dt_bias"],
              tabs["a_neg"], tabs["d"], tabs["norm"], s0],
        out_specs=[col(1, SSD_WIDTH),
                   pl.BlockSpec(st_block, lambda s, c: (layer, s, 0, 0, 0)),
                   pl.BlockSpec(cs_block, conv_idx(layer))],
        out_shapes=[jax.ShapeDtypeStruct((proj.shape[0], MIX_WIDTH), BF16),
                    jax.ShapeDtypeStruct((depth,) + s0.shape[1:], F32),
                    jax.ShapeDtypeStruct((depth,) + conv0.shape[1:], F32)],
        carried=[mix, s_out, conv_out],
        scratch=[pltpu.VMEM((nb, _SSD_PREV + t, SSD_CONV_DIM), F32),
                 pltpu.VMEM((rb, SSD_WIDTH), F32),
                 pltpu.VMEM((rb, SSD_WIDTH), F32),
                 pltpu.VMEM((rb, SSD_WIDTH), F32)],
        name="ssd_mixer",
    )


def _softmax(s):
    s = s - jnp.max(s, axis=-1, keepdims=True)
    p = jnp.exp(s)
    return p / jnp.sum(p, axis=-1, keepdims=True)


def _xattn_kernel(q_ref, k_ref, v_ref, o_ref, *, tq, nb, head_dim, heads_axis):
    if not heads_axis:
        for h in range(XATTN_HEADS):
            cols = slice(h * head_dim, (h + 1) * head_dim)
            q = q_ref[:, cols]
            pieces = []
            for n in range(nb):
                s = lax.dot_general(q, k_ref[n, :, cols].astype(BF16), _NT,
                                    preferred_element_type=F32)[n * tq:(n + 1) * tq]
                pieces.append(_bdot(_softmax(s * (head_dim ** -0.5)), v_ref[n, :, cols]))
            att = pieces[0] if nb == 1 else jnp.concatenate(pieces, axis=0)
            o_ref[:, cols] = att.astype(o_ref.dtype)
        return

    flat = MEM_TOKENS * XATTN_HEADS
    rows = XATTN_HEADS * tq
    row_head = lax.broadcasted_iota(jnp.int32, (rows, flat), 0) // tq
    col_head = lax.broadcasted_iota(jnp.int32, (rows, flat), 1) % XATTN_HEADS
    same_head = row_head == col_head
    q_all = q_ref[...].astype(F32)
    outs = []
    for n in range(nb):
        q = jnp.concatenate([q_all[n * tq:(n + 1) * tq, h * head_dim:(h + 1) * head_dim]
                             for h in range(XATTN_HEADS)], axis=0)
        k = k_ref[n].reshape(flat, head_dim)
        v = v_ref[n].reshape(flat, head_dim)
        s = lax.dot_general(q.astype(BF16), k.astype(BF16), _NT, preferred_element_type=F32)
        s = jnp.where(same_head, s * (head_dim ** -0.5), -jnp.inf)
        outs.append(_bdot(_softmax(s), v))
    for h in range(XATTN_HEADS):
        att = jnp.concatenate([o[h * tq:(h + 1) * tq] for o in outs], axis=0)
        o_ref[:, h * head_dim:(h + 1) * head_dim] = att.astype(o_ref.dtype)


def cross_attention(q, att, row0, n_seq, seq_len, mem_k, mem_v, layer=None):
    d = q.shape[-1]
    head_dim = d // XATTN_HEADS
    if seq_len > SUBLANES:
        tq, nb = 512, 1
    else:
        tq, nb = seq_len, 2
    n_q = seq_len // tq
    rb = nb * tq
    if layer is None:
        kv_spec = pl.BlockSpec((nb, MEM_TOKENS, d), lambda s, i: (s, 0, 0))
    else:
        kv_spec = pl.BlockSpec((None, nb, MEM_TOKENS, XATTN_HEADS, head_dim), lambda s, i: (layer, s, 0, 0, 0))
    q_spec = pl.BlockSpec((rb, d), lambda s, i: (row0 // rb + s * n_q + i, 0))
    (att,) = _pallas(
        functools.partial(_xattn_kernel, tq=tq, nb=nb, head_dim=head_dim, heads_axis=layer is not None),
        grid=(n_seq // nb, n_q),
        in_specs=[q_spec, kv_spec, kv_spec],
        args=[q, mem_k, mem_v],
        out_specs=[q_spec],
        out_shapes=[jax.ShapeDtypeStruct(q.shape, BF16)],
        carried=[att],
        name="cross_attention",
    )
    return att


def kernel(x_prompt, x_sample, mem_prompt, state_s5_re, state_s5_im, state_ret, state_ssm, state_conv, cache_mem_k, cache_mem_v, ffn1_norm, ffn1_w1, ffn1_w3, ffn1_w2, mix_norm, w_in, w_out, s5_lambda_re, s5_lambda_im, s5_b_re, s5_b_im, s5_c_re, s5_c_im, s5_d, s5_log_step, s5_glu_w, s5_glu_b, ret_norm, ssd_conv_w, ssd_conv_b, ssd_dt_bias, ssd_a_log, ssd_d, ssd_norm, xattn_norm, xattn_wq, xattn_wk, xattn_wv, xattn_wo, ffn2_norm, ffn2_w1, ffn2_w3, ffn2_w2, final_norm):
    bp, lp, d = x_prompt.shape
    bs, ls, _ = x_sample.shape
    depth = w_in.shape[0]
    ffn = ffn1_w1.shape[-1]
    n_p, n_s = bp * lp, bs * ls

    def gain(g):
        return g.astype(F32).reshape(-1, 1, g.shape[-1])

    def lane_pad(v):
        return jnp.pad(v.astype(F32), ((0, 0), (0, LANES - v.shape[-1]))).reshape(depth, 1, LANES)

    s5_tabs = s5_tables(s5_lambda_re, s5_lambda_im, s5_b_re, s5_b_im, s5_c_re, s5_c_im, s5_d, s5_log_step,
                        s5_glu_w, s5_glu_b)
    main_cols = w_in.shape[-1] - SSD_HEADS
    w_dt = jnp.pad(w_in[:, :, main_cols:], ((0, 0), (0, 0), (0, LANES - SSD_HEADS)))
    ssd_tabs = dict(conv_w=ssd_conv_w.astype(F32), conv_b=gain(ssd_conv_b), dt_bias=lane_pad(ssd_dt_bias),
                    a_neg=lane_pad(-jnp.exp(ssd_a_log.astype(F32))), d=lane_pad(ssd_d), norm=gain(ssd_norm))
    ret_w = gain(ret_norm)
    cos_p, sin_p = rope_tables(jnp.arange(lp, dtype=jnp.int32))
    cos_s, sin_s = rope_tables(PAST_LEN + jnp.arange(ls, dtype=jnp.int32))
    mem = mem_prompt.reshape(bp * MEM_TOKENS, d).astype(BF16)
    norms = {name: gain(g) for name, g in dict(ffn1=ffn1_norm, mix=mix_norm, xattn=xattn_norm, ffn2=ffn2_norm,
                                                final=final_norm).items()}

    ssm_t = state_ssm.swapaxes(-1, -2)
    conv_t = state_conv.swapaxes(1, 2)
    s5_re_cm = jnp.stack([_to_chunk_major(state_s5_re[l]) for l in range(depth)])
    s5_im_cm = jnp.stack([_to_chunk_major(state_s5_im[l]) for l in range(depth)])
    zeros_s5 = jnp.zeros((S5_LANE_CHUNKS, bp, LANES), F32)
    zeros_ret = jnp.zeros((1, bp) + state_ret.shape[2:], F32)
    zeros_ssm = jnp.zeros((1, bp) + ssm_t.shape[2:], F32)
    zeros_conv = jnp.zeros((1, SSD_CONV - 1, bp, SSD_CONV_DIM), F32)

    x = jnp.concatenate([x_prompt.reshape(n_p, d), x_sample.reshape(n_s, d)], axis=0)
    lists = {name: [] for name in ("p_s5re", "p_s5im", "p_mk", "p_mv", "s_s5re", "s_s5im")}
    p_ret = p_ssm = p_conv = s_ret = s_ssm = s_conv = None

    def ffn_half(x, norm, w1, w3, w2, layer):
        h = rmsnorm(x, norm, layer, BF16)
        a = matmul(h, [w1, w3], layer, n_cols=ffn, out_dtype=BF16)
        return matmul(a, [w2], layer, n_cols=d, out_dtype=F32, res=x, scale=0.5)

    for l in range(depth):
        x = ffn_half(x, norms["ffn1"], ffn1_w1, ffn1_w3, ffn1_w2, l)

        h = rmsnorm(x, norms["mix"], l, BF16)
        proj = matmul(h, [w_in], l, n_cols=main_cols, out_dtype=F32)
        dt_raw = matmul(h, [w_dt], l, n_cols=LANES, out_dtype=F32)

        mix, hr_p, hi_p = s5_mixer(proj, None, 0, bp, lp, zeros_s5, zeros_s5, s5_tabs, l)
        mix, hr_s, hi_s = s5_mixer(proj, mix, n_p, bs, ls, s5_re_cm[l], s5_im_cm[l], s5_tabs, l)
        mix, p_ret = retention_mixer(proj, mix, 0, bp, lp, zeros_ret, 0, p_ret, depth, cos_p, sin_p, ret_w, l)
        mix, s_ret = retention_mixer(proj, mix, n_p, bs, ls, state_ret, l, s_ret, depth, cos_s, sin_s, ret_w, l)
        mix, p_ssm, p_conv = ssd_mixer(proj, dt_raw, mix, 0, bp, lp, zeros_ssm, zeros_conv, 0, p_ssm, p_conv,
                                       depth, ssd_tabs, l)
        mix, s_ssm, s_conv = ssd_mixer(proj, dt_raw, mix, n_p, bs, ls, ssm_t, conv_t, l, s_ssm, s_conv,
                                       depth, ssd_tabs, l)
        x = matmul(mix, [w_out], l, n_cols=d, out_dtype=F32, res=x)

        h = rmsnorm(x, norms["xattn"], l, BF16)
        q = matmul(h, [xattn_wq], l, n_cols=d, out_dtype=BF16)
        mk = matmul(mem, [xattn_wk], l, n_cols=d, out_dtype=F32)
        mv = matmul(mem, [xattn_wv], l, n_cols=d, out_dtype=F32)
        att = cross_attention(q, None, 0, bp, lp, mk.astype(BF16).reshape(bp, MEM_TOKENS, d),
                              mv.astype(BF16).reshape(bp, MEM_TOKENS, d))
        att = cross_attention(q, att, n_p, bs, ls, cache_mem_k, cache_mem_v, layer=l)
        x = matmul(att, [xattn_wo], l, n_cols=d, out_dtype=F32, res=x)

        x = ffn_half(x, norms["ffn2"], ffn2_w1, ffn2_w3, ffn2_w2, l)

        lists["p_s5re"].append(_from_chunk_major(hr_p))
        lists["p_s5im"].append(_from_chunk_major(hi_p))
        lists["p_mk"].append(mk.reshape(bp, MEM_TOKENS, XATTN_HEADS, d // XATTN_HEADS))
        lists["p_mv"].append(mv.reshape(bp, MEM_TOKENS, XATTN_HEADS, d // XATTN_HEADS))
        lists["s_s5re"].append(_from_chunk_major(hr_s))
        lists["s_s5im"].append(_from_chunk_major(hi_s))

    y = rmsnorm(x, norms["final"], 0, F32)
    st = {name: jnp.stack(v) for name, v in lists.items()}
    return (y[:n_p].reshape(bp, lp, d), y[n_p:].reshape(bs, ls, d),
            st["p_s5re"], st["p_s5im"], p_ret, p_ssm.swapaxes(-1, -2), p_conv.swapaxes(1, 2),
            st["p_mk"], st["p_mv"],
            st["s_s5re"], st["s_s5im"], s_ret, s_ssm.swapaxes(-1, -2), s_conv.swapaxes(1, 2))
```

```python
import functools
import math

import jax
import jax.numpy as jnp
from jax import lax
from jax.experimental import pallas as pl
from jax.experimental.pallas import tpu as pltpu

F32 = jnp.float32
BF16 = jnp.bfloat16

EPS = 1e-6
S5_GROUPS = 32
S5_GROUP_CH = 16
S5_STATE = 64
S5_MODES = S5_GROUPS * S5_STATE
RET_HEADS = 4
RET_HEAD_DIM = 128
ROPE_BASE = 10000.0
SSD_HEADS = 16
SSD_HEAD_DIM = 64
SSD_GROUPS = 2
SSD_STATE = 128
SSD_CONV = 4
MEM_TOKENS = 256
XATTN_HEADS = 4
CHUNK = 128
PAST_LEN = 16384

MIX_WIDTH = 2048

LANES = 128
SUBLANES = 8
MIB = 1024 * 1024
VMEM_LIMIT_BYTES = 56 * MIB
MATMUL_VMEM_BUDGET = VMEM_LIMIT_BYTES - 10 * MIB

_NT = (((1,), (1,)), ((), ()))
_TN = (((0,), (0,)), ((), ()))


def _params(n_axes):
    return pltpu.CompilerParams(dimension_semantics=("arbitrary",) * n_axes,
                                vmem_limit_bytes=VMEM_LIMIT_BYTES)


def _pallas(body, *, grid, in_specs, args, out_specs, out_shapes, carried=(), scratch=(), name):
    carried = list(carried) + [None] * (len(out_shapes) - len(carried))
    extra = [(i, c) for i, c in enumerate(carried) if c is not None]
    n_in = len(args)

    def with_carried(*refs):
        body(*refs[:n_in], *refs[n_in + len(extra):])

    return pl.pallas_call(
        with_carried,
        grid=grid,
        in_specs=list(in_specs) + [pl.BlockSpec(memory_space=pl.ANY)] * len(extra),
        out_specs=out_specs,
        out_shape=out_shapes,
        input_output_aliases={n_in + k: i for k, (i, _) in enumerate(extra)},
        scratch_shapes=list(scratch),
        compiler_params=_params(len(grid)),
        name=name,
    )(*args, *[c for _, c in extra])


def _bdot(a, b):
    return jnp.dot(a.astype(BF16), b.astype(BF16), preferred_element_type=F32)


def _split3(x):
    x1 = x.astype(BF16)
    r1 = x - x1.astype(F32)
    x2 = r1.astype(BF16)
    x3 = (r1 - x2.astype(F32)).astype(BF16)
    return x1, x2, x3


def _rmsnorm_kernel(x_ref, g_ref, o_ref):
    x = x_ref[...]
    ms = jnp.mean(x * x, axis=-1, keepdims=True)
    o_ref[...] = ((x * lax.rsqrt(ms + EPS)) * g_ref[...]).astype(o_ref.dtype)


def rmsnorm(x, gains, layer, out_dtype, row0=0, n_rows=None, tm=512):
    d = x.shape[1]
    m = x.shape[0] if n_rows is None else n_rows
    return pl.pallas_call(
        _rmsnorm_kernel,
        grid=(m // tm,),
        in_specs=[pl.BlockSpec((tm, d), lambda i: (row0 // tm + i, 0)),
                  pl.BlockSpec((None, 1, d), lambda i: (layer, 0, 0))],
        out_specs=pl.BlockSpec((tm, d), lambda i: (i, 0)),
        out_shape=jax.ShapeDtypeStruct((m, d), out_dtype),
        compiler_params=_params(1),
        name="rmsnorm",
    )(x, gains)


def _mm_kernel(*refs, n_w, has_res, scale, w_transposed):
    a_ref = refs[0]
    w_refs = refs[1:1 + n_w]
    res_ref = refs[1 + n_w] if has_res else None
    o_ref = refs[1 + n_w + int(has_res)]
    wb_refs = refs[2 + n_w + int(has_res):]

    @pl.when(pl.program_id(1) == 0)
    def _():
        for w_ref, wb_ref in zip(w_refs, wb_refs):
            wb_ref[...] = w_ref[...].astype(BF16)

    def mm(a, wb_ref):
        if w_transposed:
            return lax.dot_general(a, wb_ref[...], _NT, preferred_element_type=F32)
        return jnp.dot(a, wb_ref[...], preferred_element_type=F32)

    a = a_ref[...]
    acc = mm(a, wb_refs[0])
    if n_w == 2:
        acc = jax.nn.silu(acc) * mm(a, wb_refs[1])
    if has_res:
        acc = res_ref[...] + (acc if scale == 1.0 else scale * acc)
    o_ref[...] = acc.astype(o_ref.dtype)


def _matmul_tiles(m, k, n_cols, n_w, has_res, out_bytes):
    for tn in (1024, 512, 256, 128):
        if n_cols % tn:
            continue
        for tm in (1024, 512, 256):
            if m % tm:
                continue
            need = (2 * tm * k * 2
                    + n_w * (2 * k * tn * 4 + k * tn * 2)
                    + 2 * tm * tn * out_bytes + (2 * tm * tn * 4 if has_res else 0)
                    + n_w * tm * tn * 4)
            if need <= MATMUL_VMEM_BUDGET:
                return tn, tm
    raise ValueError("no matmul tiling fits VMEM")


def matmul(a, ws, layer, *, n_cols, out_dtype, res=None, scale=1.0, w_transposed=False):
    m, k = a.shape
    n_w = len(ws)
    tn, tm = _matmul_tiles(m, k, n_cols, n_w, res is not None, jnp.dtype(out_dtype).itemsize)
    in_specs = [pl.BlockSpec((tm, k), lambda j, i: (i, 0))]
    if w_transposed:
        in_specs += [pl.BlockSpec((None, tn, k), lambda j, i: (layer, j, 0)) for _ in ws]
    else:
        in_specs += [pl.BlockSpec((None, k, tn), lambda j, i: (layer, 0, j)) for _ in ws]
    args = [a, *ws]
    if res is not None:
        in_specs.append(pl.BlockSpec((tm, tn), lambda j, i: (i, j)))
        args.append(res)
    return pl.pallas_call(
        functools.partial(_mm_kernel, n_w=n_w, has_res=res is not None, scale=scale, w_transposed=w_transposed),
        grid=(n_cols // tn, m // tm),
        in_specs=in_specs,
        out_specs=pl.BlockSpec((tm, tn), lambda j, i: (i, j)),
        out_shape=jax.ShapeDtypeStruct((m, n_cols), out_dtype),
        scratch_shapes=[pltpu.VMEM((tn, k) if w_transposed else (k, tn), BF16) for _ in ws],
        compiler_params=_params(2),
        name="matmul",
    )(*args)


S5_LANE_CHUNKS = S5_MODES // LANES
S5_POWERS = 8
S5_WIDTH = S5_GROUPS * S5_GROUP_CH
S5_SUPER = 4
S5_SUPER_CH = S5_WIDTH // S5_SUPER
S5_SUPER_MODES = S5_MODES // S5_SUPER
S5_SUPER_CHUNKS = S5_SUPER_MODES // LANES
S5_BLOCK_ROWS = 256


def _cmul_add(xr, xi, ar, ai, br, bi, keep=None):
    pr = ar * br - ai * bi
    pi = ar * bi + ai * br
    if keep is not None:
        pr = jnp.where(keep, pr, 0.0)
        pi = jnp.where(keep, pi, 0.0)
    return xr + pr, xi + pi


def _s5_kernel(u_ref, h0r_ref, h0i_ref, pwr_ref, pwi_ref, lnr_ref, lni_ref, bsup_ref, csup_ref, d_ref, gw_ref,
               gb_ref, o_ref, hor_ref, hoi_ref, bur_ref, bui_ref, hs_ref, *, rb, chain):
    ntile = rb // SUBLANES
    u = u_ref[...]
    ub = u.astype(BF16)
    for q in range(S5_SUPER):
        bu = jnp.dot(ub[:, q * S5_SUPER_CH:(q + 1) * S5_SUPER_CH], bsup_ref[q], preferred_element_type=F32)
        for r in range(S5_SUPER_CHUNKS):
            bur_ref[q * S5_SUPER_CHUNKS + r] = bu[:, r * LANES:(r + 1) * LANES]
            bui_ref[q * S5_SUPER_CHUNKS + r] = bu[:, S5_SUPER_MODES + r * LANES:S5_SUPER_MODES + (r + 1) * LANES]

    if chain:
        @pl.when(pl.program_id(1) == 0)
        def _():
            hor_ref[...] = h0r_ref[...]
            hoi_ref[...] = h0i_ref[...]

    step_in_tile = lax.broadcasted_iota(jnp.int32, (ntile, SUBLANES, LANES), 1)
    tile_idx = lax.broadcasted_iota(jnp.int32, (ntile, LANES), 0)
    tile_last = pl.ds(SUBLANES - 1, ntile, stride=SUBLANES)
    tile_steps = SUBLANES.bit_length() - 1

    def chunk(c, carry):
        pr = pwr_ref[c]
        pi = pwi_ref[c]
        xr = bur_ref[c].reshape(ntile, SUBLANES, LANES)
        xi = bui_ref[c].reshape(ntile, SUBLANES, LANES)
        for k in range(tile_steps):
            sh = 1 << k
            xr, xi = _cmul_add(xr, xi, pr[k:k + 1], pi[k:k + 1], pltpu.roll(xr, sh, 1), pltpu.roll(xi, sh, 1),
                               step_in_tile >= sh)
        p8r, p8i = pr[tile_steps:tile_steps + 1], pi[tile_steps:tile_steps + 1]
        if chain:
            bur_ref[c] = xr.reshape(rb, LANES)
            bui_ref[c] = xi.reshape(rb, LANES)
            tr = bur_ref[c, tile_last, :]
            ti = bui_ref[c, tile_last, :]
            hr, hi = hor_ref[c], hoi_ref[c]
            first = tile_idx == 0
            cr, ci = _cmul_add(tr, ti, p8r, p8i, hr, hi, first)
            for j in range(ntile.bit_length() - 1):
                sh = 1 << j
                k = tile_steps + j
                cr, ci = _cmul_add(cr, ci, pr[k:k + 1], pi[k:k + 1], pltpu.roll(cr, sh, 0), pltpu.roll(ci, sh, 0),
                                   tile_idx >= sh)
            prev_r = jnp.where(first, hr, pltpu.roll(cr, 1, 0))
            prev_i = jnp.where(first, hi, pltpu.roll(ci, 1, 0))
            hor_ref[c] = cr[ntile - 1:ntile]
            hoi_ref[c] = ci[ntile - 1:ntile]
        else:
            prev_r, prev_i = h0r_ref[c], h0i_ref[c]
        lr = lnr_ref[c]
        li = lni_ref[c]
        for k in range(ntile):
            rows = slice(k * SUBLANES, (k + 1) * SUBLANES)
            yr, yi = _cmul_add(xr[k], xi[k], lr, li, prev_r[k:k + 1], prev_i[k:k + 1])
            bur_ref[c, rows, :] = yr
            bui_ref[c, rows, :] = yi
        if not chain:
            hor_ref[c] = bur_ref[c, tile_last, :]
            hoi_ref[c] = bui_ref[c, tile_last, :]
        col = pl.multiple_of((c // S5_SUPER_CHUNKS) * 2 * S5_SUPER_MODES + (c % S5_SUPER_CHUNKS) * LANES, LANES)
        hs_ref[:, pl.ds(col, LANES)] = bur_ref[c].astype(BF16)
        hs_ref[:, pl.ds(col + S5_SUPER_MODES, LANES)] = bui_ref[c].astype(BF16)
        return carry

    lax.fori_loop(0, S5_LANE_CHUNKS, chunk, 0)

    y = jnp.concatenate(
        [jnp.dot(hs_ref[:, q * 2 * S5_SUPER_MODES:(q + 1) * 2 * S5_SUPER_MODES], csup_ref[q],
                 preferred_element_type=F32) for q in range(S5_SUPER)], axis=1) + d_ref[...] * u
    y = jax.nn.gelu(y)
    gate = jax.nn.sigmoid(_bdot(y, gw_ref[...]) + gb_ref[...])
    o_ref[...] = (y * gate).astype(o_ref.dtype)


def s5_mixer(proj, mix, row0, n_seq, seq_len, h0r, h0i, tabs, layer):
    chain = seq_len > SUBLANES
    rb = S5_BLOCK_ROWS
    if chain:
        grid = (n_seq, seq_len // rb)
        rows = lambda s, b: (row0 // rb + s * (seq_len // rb) + b, 0)
        h0r = h0r.reshape(S5_LANE_CHUNKS, n_seq, 1, LANES)
        h0i = h0i.reshape(S5_LANE_CHUNKS, n_seq, 1, LANES)
        st_spec = pl.BlockSpec((S5_LANE_CHUNKS, None, 1, LANES), lambda s, b: (0, s, 0, 0))
        st_shape = jax.ShapeDtypeStruct((S5_LANE_CHUNKS, n_seq, 1, LANES), F32)
    else:
        assert seq_len == SUBLANES
        nseg = rb // seq_len
        grid = (n_seq // nseg, 1)
        rows = lambda s, b: (row0 // rb + s, 0)
        st_spec = pl.BlockSpec((S5_LANE_CHUNKS, nseg, LANES), lambda s, b: (0, s, 0))
        st_shape = jax.ShapeDtypeStruct((S5_LANE_CHUNKS, n_seq, LANES), F32)
    const3 = lambda s, b: (layer, 0, 0)
    const4 = lambda s, b: (layer, 0, 0, 0)
    pw_spec = pl.BlockSpec((None, S5_LANE_CHUNKS, S5_POWERS, LANES), const4)
    lin_spec = pl.BlockSpec((None, S5_LANE_CHUNKS, SUBLANES, LANES), const4)
    mix, hr, hi = _pallas(
        functools.partial(_s5_kernel, rb=rb, chain=chain),
        grid=grid,
        in_specs=[pl.BlockSpec((rb, S5_WIDTH), rows), st_spec, st_spec, pw_spec, pw_spec, lin_spec, lin_spec,
                  pl.BlockSpec((None, S5_SUPER, S5_SUPER_CH, 2 * S5_SUPER_MODES), const4),
                  pl.BlockSpec((None, S5_SUPER, 2 * S5_SUPER_MODES, S5_SUPER_CH), const4),
                  pl.BlockSpec((None, 1, S5_WIDTH), const3),
                  pl.BlockSpec((None, S5_WIDTH, S5_WIDTH), const3),
                  pl.BlockSpec((None, 1, S5_WIDTH), const3)],
        args=[proj, h0r, h0i, tabs["pw_re"], tabs["pw_im"], tabs["lin_re"], tabs["lin_im"], tabs["bsup"],
              tabs["csup"], tabs["d"], tabs["glu_w"], tabs["glu_b"]],
        out_specs=[pl.BlockSpec((rb, S5_WIDTH), rows), st_spec, st_spec],
        out_shapes=[jax.ShapeDtypeStruct((proj.shape[0], MIX_WIDTH), BF16), st_shape, st_shape],
        carried=[mix],
        scratch=[pltpu.VMEM((S5_LANE_CHUNKS, rb, LANES), F32),
                 pltpu.VMEM((S5_LANE_CHUNKS, rb, LANES), F32),
                 pltpu.VMEM((rb, 2 * S5_MODES), BF16)],
        name="s5_mixer",
    )
    return mix, hr.reshape(S5_LANE_CHUNKS, n_seq, LANES), hi.reshape(S5_LANE_CHUNKS, n_seq, LANES)


def s5_tables(lam_re, lam_im, b_re, b_im, c_re, c_im, d, log_step, glu_w, glu_b):
    depth = lam_re.shape[0]
    step = jnp.exp(log_step.astype(F32))[..., None]
    lr, li = lam_re.astype(F32), lam_im.astype(F32)
    mag = jnp.exp(lr * step)
    lbr, lbi = mag * jnp.cos(li * step), mag * jnp.sin(li * step)
    den = lr * lr + li * li
    fr = ((lbr - 1.0) * lr + lbi * li) / den
    fi = (lbi * lr - (lbr - 1.0) * li) / den
    bbr = fr[..., None] * b_re - fi[..., None] * b_im
    bbi = fr[..., None] * b_im + fi[..., None] * b_re
    per = S5_GROUPS // S5_SUPER
    eye = jnp.eye(per, dtype=F32)

    def blk_in(t):
        t = t.reshape(depth, S5_SUPER, per, S5_STATE, S5_GROUP_CH)
        return jnp.einsum("dqgpc,gh->dqgchp", t, eye).reshape(depth, S5_SUPER, S5_SUPER_CH, S5_SUPER_MODES)

    def blk_out(t):
        t = t.reshape(depth, S5_SUPER, per, S5_GROUP_CH, S5_STATE)
        return jnp.einsum("dqgcp,gh->dqgphc", t, eye).reshape(depth, S5_SUPER, S5_SUPER_MODES, S5_SUPER_CH)

    bsup = jnp.concatenate([blk_in(bbr), blk_in(bbi)], axis=-1).astype(BF16)
    csup = jnp.concatenate([blk_out(c_re.astype(F32)), blk_out(-c_im.astype(F32))], axis=2).astype(BF16)
    pr, pi = lbr.reshape(depth, S5_MODES), lbi.reshape(depth, S5_MODES)
    prs, pis = [], []
    for _ in range(S5_POWERS):
        prs.append(pr)
        pis.append(pi)
        pr, pi = pr * pr - pi * pi, 2.0 * pr * pi
    lr, li = prs[0], pis[0]
    lrs, lis = [], []
    for _ in range(SUBLANES):
        lrs.append(lr)
        lis.append(li)
        lr, li = lr * prs[0] - li * pis[0], lr * pis[0] + li * prs[0]

    def chunked(ts):
        t = jnp.stack(ts, axis=1).reshape(depth, len(ts), S5_LANE_CHUNKS, LANES)
        return t.transpose(0, 2, 1, 3)

    return dict(pw_re=chunked(prs), pw_im=chunked(pis), lin_re=chunked(lrs), lin_im=chunked(lis),
                bsup=bsup, csup=csup, d=d.astype(F32).reshape(depth, 1, S5_WIDTH), glu_w=glu_w.astype(BF16),
                glu_b=glu_b.astype(F32).reshape(depth, 1, S5_WIDTH))


def _to_chunk_major(h):
    n = h.shape[0]
    return h.reshape(n, S5_LANE_CHUNKS, LANES).transpose(1, 0, 2)


def _from_chunk_major(h):
    n = h.shape[1]
    return h.transpose(1, 0, 2).reshape(n, S5_GROUPS, S5_STATE)


_RET_LOG_GAMMA = tuple(math.log(1.0 - 2.0 ** (-5.0 - h)) for h in range(RET_HEADS))
RET_WIDTH = RET_HEADS * RET_HEAD_DIM


def _ret_kernel(q_ref, k_ref, v_ref, g_ref, cos_ref, sin_ref, nw_ref, s0_ref, o_ref, s_ref, y_ref, *, t, nb):
    @pl.when(pl.program_id(1) == 0)
    def _():
        s_ref[...] = s0_ref[...]

    cos = cos_ref[...]
    sin = sin_ref[...]
    ii = lax.broadcasted_iota(jnp.int32, (t, t), 0)
    jj = lax.broadcasted_iota(jnp.int32, (t, t), 1)
    ti = lax.broadcasted_iota(jnp.int32, (t, 1), 0).astype(F32)
    half = RET_HEAD_DIM // 2
    for h in range(RET_HEADS):
        lg = _RET_LOG_GAMMA[h]
        cols = slice(h * RET_HEAD_DIM, (h + 1) * RET_HEAD_DIM)
        decay = jnp.exp(jnp.where(ii >= jj, (ii - jj).astype(F32) * lg, -jnp.inf))
        grow = jnp.exp((ti + 1.0) * lg)
        tail = jnp.exp((t - 1.0 - ti) * lg)
        for n in range(nb):
            rows = slice(n * t, (n + 1) * t)
            q = q_ref[rows, cols]
            k = k_ref[rows, cols]
            v = v_ref[rows, cols].astype(BF16)
            q = q * cos + pltpu.roll(q, half, 1) * sin
            k = (k * cos + pltpu.roll(k, half, 1) * sin) * (RET_HEAD_DIM ** -0.5)
            s_prev = s_ref[n, h]
            scores = lax.dot_general(q.astype(BF16), k.astype(BF16), _NT, preferred_element_type=F32) * decay
            y = _bdot(scores, v) + _bdot(q * grow, s_prev)
            s_ref[n, h] = math.exp(t * lg) * s_prev + lax.dot_general(
                (k * tail).astype(BF16), v, _TN, preferred_element_type=F32)
            y = y * lax.rsqrt(jnp.mean(y * y, axis=-1, keepdims=True) + EPS)
            y = y * nw_ref[:, cols]
            y_ref[rows, cols] = jax.nn.silu(g_ref[rows, cols]) * y
    o_ref[...] = y_ref[...].astype(o_ref.dtype)


def retention_mixer(proj, mix, row0, n_seq, seq_len, s0, s0_layer, s_out, depth, cos2, sin2, norm_w, layer):
    t = math.gcd(seq_len, CHUNK)
    n_chunks = seq_len // t
    nb = 1 if n_chunks > 1 else 8
    rb = nb * t

    def col(cb):
        return pl.BlockSpec((rb, RET_WIDTH), lambda s, c: (row0 // rb + s * n_chunks + c, cb))

    st_block = (None, nb, RET_HEADS, RET_HEAD_DIM, RET_HEAD_DIM)
    tab_spec = pl.BlockSpec((t, RET_HEAD_DIM), lambda s, c: (c, 0))
    return _pallas(
        functools.partial(_ret_kernel, t=t, nb=nb),
        grid=(n_seq // nb, n_chunks),
        in_specs=[col(1), col(2), col(3), col(4), tab_spec, tab_spec,
                  pl.BlockSpec((None, 1, RET_WIDTH), lambda s, c: (layer, 0, 0)),
                  pl.BlockSpec(st_block, lambda s, c: (s0_layer, s, 0, 0, 0))],
        args=[proj, proj, proj, proj, cos2, sin2, norm_w, s0],
        out_specs=[col(1), pl.BlockSpec(st_block, lambda s, c: (layer, s, 0, 0, 0))],
        out_shapes=[jax.ShapeDtypeStruct((proj.shape[0], MIX_WIDTH), BF16),
                    jax.ShapeDtypeStruct((depth,) + s0.shape[1:], F32)],
        carried=[mix, s_out],
        scratch=[pltpu.VMEM((rb, RET_WIDTH), F32)],
        name="retention_mixer",
    )


def rope_tables(pos):
    half = RET_HEAD_DIM // 2
    inv = ROPE_BASE ** (-jnp.arange(half, dtype=F32) / half)
    ang = pos.astype(F32)[:, None] * inv[None, :]
    cos, sin = jnp.cos(ang), jnp.sin(ang)
    return jnp.concatenate([cos, cos], axis=-1), jnp.concatenate([-sin, sin], axis=-1)


SSD_WIDTH = SSD_HEADS * SSD_HEAD_DIM
SSD_BC = SSD_GROUPS * SSD_STATE
SSD_CONV_DIM = SSD_WIDTH + 2 * SSD_BC
_SSD_PREV = SUBLANES
_COL_BLOCK = 512
_SSD_Z_COL = 2560
_SSD_XBC_COL = _SSD_Z_COL + SSD_WIDTH


def _ssd_kernel(z0_ref, z1_ref, x0_ref, x1_ref, x2_ref, dt_ref, cs_ref, cw_ref, cb_ref, dtb_ref, aneg_ref, dsk_ref,
                nw_ref, s0_ref, o_ref, s_ref, co_ref, ext_ref, xbc_ref, y_ref, xdt_ref, xtl_ref, fs_ref, *,
                t, nb, conv_rows):
    keep = SSD_CONV - 1
    rep = SSD_HEADS // SSD_GROUPS
    grp = rep * SSD_HEAD_DIM

    def conv_row(n):
        return pl.ds(n, 1) if conv_rows == nb else pl.ds(pl.program_id(0) * nb + n, 1)

    @pl.when(pl.program_id(1) == 0)
    def _():
        s_ref[...] = s0_ref[...]
        for n in range(nb):
            for r in range(keep):
                ext_ref[n, _SSD_PREV - keep + r:_SSD_PREV - keep + r + 1, :] = cs_ref[r, conv_row(n), :]

    rb = nb * t
    ii = lax.broadcasted_iota(jnp.int32, (rb, rb), 0)
    jj = lax.broadcasted_iota(jnp.int32, (rb, rb), 1)
    same_seq = (ii // t) == (jj // t)
    causal = jnp.logical_and(same_seq, ii >= jj)
    tri = causal.astype(BF16)
    seq_ones = same_seq.astype(BF16)
    cw = cw_ref[...]
    for n in range(nb):
        rows = slice(n * t, (n + 1) * t)
        ext_ref[n, _SSD_PREV:_SSD_PREV + t, 0:_COL_BLOCK] = x0_ref[rows, :]
        ext_ref[n, _SSD_PREV:_SSD_PREV + t, _COL_BLOCK:2 * _COL_BLOCK] = x1_ref[rows, :]
        ext_ref[n, _SSD_PREV:_SSD_PREV + t, 2 * _COL_BLOCK:] = x2_ref[rows, :]
        conv = cb_ref[...] + cw[keep:keep + 1] * ext_ref[n, _SSD_PREV:_SSD_PREV + t, :]
        for back in range(1, SSD_CONV):
            conv = conv + cw[keep - back:keep - back + 1] * ext_ref[n, pl.ds(_SSD_PREV - back, t), :]
        tail_rows = ext_ref[n, _SSD_PREV + t - keep:_SSD_PREV + t, :]
        for r in range(keep):
            co_ref[r, conv_row(n), :] = tail_rows[r:r + 1]
        ext_ref[n, _SSD_PREV - keep:_SSD_PREV, :] = tail_rows
        xbc_ref[rows, :] = jax.nn.silu(conv)

    dt = jax.nn.softplus(dt_ref[...] + dtb_ref[...])
    la = dt * aneg_ref[...]
    pieces = _split3(la)
    cum = sum(jnp.dot(tri, p, preferred_element_type=F32) for p in pieces)
    tot = sum(jnp.dot(seq_ones, p, preferred_element_type=F32) for p in pieces)
    cum_t = cum.T
    e_last = jnp.exp(tot)

    spread = (lax.broadcasted_iota(jnp.int32, (LANES, SSD_WIDTH), 0)
              == lax.broadcasted_iota(jnp.int32, (LANES, SSD_WIDTH), 1) // SSD_HEAD_DIM).astype(BF16)

    def per_column(v):
        return sum(jnp.dot(p, spread, preferred_element_type=F32) for p in _split3(v))

    xh = xbc_ref[:, :SSD_WIDTH]
    xdt_ref[...] = xh * per_column(dt)
    xtl_ref[...] = xh * per_column(dt * jnp.exp(tot - cum))
    e_cum_cols = per_column(jnp.exp(cum))
    first_of_pair = lax.broadcasted_iota(jnp.int32, (1, 2 * SSD_HEAD_DIM), 1) < SSD_HEAD_DIM
    for g in range(SSD_GROUPS):
        gc = slice(g * grp, (g + 1) * grp)
        bg = xbc_ref[:, SSD_WIDTH + g * SSD_STATE:SSD_WIDTH + (g + 1) * SSD_STATE]
        cg = xbc_ref[:, SSD_WIDTH + SSD_BC + g * SSD_STATE:SSD_WIDTH + SSD_BC + (g + 1) * SSD_STATE]
        gram = lax.dot_general(cg.astype(BF16), bg.astype(BF16), _NT, preferred_element_type=F32)
        for n in range(nb):
            rows = slice(n * t, (n + 1) * t)
            s_grp = s_ref[n, g * rep:(g + 1) * rep].reshape(grp, SSD_STATE)
            fs_ref[rows, gc] = lax.dot_general(cg[rows].astype(BF16), s_grp.astype(BF16), _NT,
                                               preferred_element_type=F32)
            s_add = lax.dot_general(xtl_ref[rows, gc].astype(BF16), bg[rows].astype(BF16), _TN,
                                    preferred_element_type=F32)
            for hh in range(rep):
                h = g * rep + hh
                s_ref[n, h] = (e_last[n * t:n * t + 1, h:h + 1] * s_ref[n, h]
                               + s_add[hh * SSD_HEAD_DIM:(hh + 1) * SSD_HEAD_DIM, :])
        for pair in range(rep // 2):
            h = g * rep + 2 * pair
            pc = slice(h * SSD_HEAD_DIM, (h + 2) * SSD_HEAD_DIM)
            xp = xdt_ref[:, pc]
            intra = 0.0
            for k in range(2):
                decay = jnp.exp(jnp.where(causal, cum[:, h + k:h + k + 1] - cum_t[h + k:h + k + 1, :], -jnp.inf))
                own = first_of_pair if k == 0 else jnp.logical_not(first_of_pair)
                intra = intra + _bdot(gram * decay, jnp.where(own, xp, 0.0))
            y_ref[:, pc] = intra
    y = y_ref[...] + e_cum_cols * fs_ref[...] + xh * dsk_ref[...]
    y = jnp.concatenate([y[:, :_COL_BLOCK] * jax.nn.silu(z0_ref[...]),
                         y[:, _COL_BLOCK:] * jax.nn.silu(z1_ref[...])], axis=1)
    y = y * lax.rsqrt(jnp.mean(y * y, axis=-1, keepdims=True) + EPS)
    o_ref[...] = (y * nw_ref[...]).astype(o_ref.dtype)


def ssd_mixer(proj, dt_raw, mix, row0, n_seq, seq_len, s0, conv0, st_layer, s_out, conv_out, depth, tabs, layer):
    t = math.gcd(seq_len, CHUNK)
    n_chunks = seq_len // t
    nb = 1 if n_chunks > 1 else SUBLANES
    rb = nb * t
    conv_rows = nb if nb % SUBLANES == 0 else n_seq

    def col(cb, width=_COL_BLOCK):
        return pl.BlockSpec((rb, width), lambda s, c: (row0 // rb + s * n_chunks + c, cb))

    def conv_idx(lyr):
        return lambda s, c: (lyr, 0, s if conv_rows == nb else 0, 0)

    z_cb = _SSD_Z_COL // _COL_BLOCK
    x_cb = _SSD_XBC_COL // _COL_BLOCK
    const3 = lambda s, c: (layer, 0, 0)
    st_block = (None, nb, SSD_HEADS, SSD_HEAD_DIM, SSD_STATE)
    cs_block = (None, SSD_CONV - 1, conv_rows, SSD_CONV_DIM)
    return _pallas(
        functools.partial(_ssd_kernel, t=t, nb=nb, conv_rows=conv_rows),
        grid=(n_seq // nb, n_chunks),
        in_specs=[col(z_cb), col(z_cb + 1), col(x_cb), col(x_cb + 1), col(x_cb + 2), col(0, LANES),
                  pl.BlockSpec(cs_block, conv_idx(st_layer)),
                  pl.BlockSpec((None, SSD_CONV, SSD_CONV_DIM), const3),
                  pl.BlockSpec((None, 1, SSD_CONV_DIM), const3),
                  pl.BlockSpec((None, 1, LANES), const3),
                  pl.BlockSpec((None, 1, LANES), const3),
                  pl.BlockSpec((None, 1, SSD_WIDTH), const3),
                  pl.BlockSpec((None, 1, SSD_WIDTH), const3),
                  pl.BlockSpec(st_block, lambda s, c: (st_layer, s, 0, 0, 0))],
        args=[proj, proj, proj, proj, proj, dt_raw, conv0, tabs["conv_w"], tabs["conv_b"], tabs["dt_bias"],
              tabs["a_neg"], tabs["d"], tabs["norm"], s0],
        out_specs=[col(1, SSD_WIDTH),
                   pl.BlockSpec(st_block, lambda s, c: (layer, s, 0, 0, 0)),
                   pl.BlockSpec(cs_block, conv_idx(layer))],
        out_shapes=[jax.ShapeDtypeStruct((proj.shape[0], MIX_WIDTH), BF16),
                    jax.ShapeDtypeStruct((depth,) + s0.shape[1:], F32),
                    jax.ShapeDtypeStruct((depth,) + conv0.shape[1:], F32)],
        carried=[mix, s_out, conv_out],
        scratch=[pltpu.VMEM((nb, _SSD_PREV + t, SSD_CONV_DIM), F32),
                 pltpu.VMEM((rb, SSD_CONV_DIM), F32)] + [pltpu.VMEM((rb, SSD_WIDTH), F32)] * 4,
        name="ssd_mixer",
    )


def _softmax(s):
    s = s - jnp.max(s, axis=-1, keepdims=True)
    p = jnp.exp(s)
    return p / jnp.sum(p, axis=-1, keepdims=True)


def _xattn_kernel(q_ref, k_ref, v_ref, o_ref, *, tq, nb, head_dim, heads_axis):
    if not heads_axis:
        for h in range(XATTN_HEADS):
            cols = slice(h * head_dim, (h + 1) * head_dim)
            q = q_ref[:, cols]
            pieces = []
            for n in range(nb):
                s = lax.dot_general(q, k_ref[n, :, cols].astype(BF16), _NT,
                                    preferred_element_type=F32)[n * tq:(n + 1) * tq]
                pieces.append(_bdot(_softmax(s * (head_dim ** -0.5)), v_ref[n, :, cols]))
            att = pieces[0] if nb == 1 else jnp.concatenate(pieces, axis=0)
            o_ref[:, cols] = att.astype(o_ref.dtype)
        return

    flat = MEM_TOKENS * XATTN_HEADS
    rows = XATTN_HEADS * tq
    row_head = lax.broadcasted_iota(jnp.int32, (rows, flat), 0) // tq
    col_head = lax.broadcasted_iota(jnp.int32, (rows, flat), 1) % XATTN_HEADS
    same_head = row_head == col_head
    q_all = q_ref[...].astype(F32)
    outs = []
    for n in range(nb):
        q = jnp.concatenate([q_all[n * tq:(n + 1) * tq, h * head_dim:(h + 1) * head_dim]
                             for h in range(XATTN_HEADS)], axis=0)
        k = k_ref[n].reshape(flat, head_dim)
        v = v_ref[n].reshape(flat, head_dim)
        s = lax.dot_general(q.astype(BF16), k.astype(BF16), _NT, preferred_element_type=F32)
        s = jnp.where(same_head, s * (head_dim ** -0.5), -jnp.inf)
        outs.append(_bdot(_softmax(s), v))
    for h in range(XATTN_HEADS):
        att = jnp.concatenate([o[h * tq:(h + 1) * tq] for o in outs], axis=0)
        o_ref[:, h * head_dim:(h + 1) * head_dim] = att.astype(o_ref.dtype)


def cross_attention(q, att, row0, n_seq, seq_len, mem_k, mem_v, layer=None):
    d = q.shape[-1]
    head_dim = d // XATTN_HEADS
    if seq_len > SUBLANES:
        tq, nb = 512, 1
    else:
        tq, nb = seq_len, 2
    n_q = seq_len // tq
    rb = nb * tq
    if layer is None:
        kv_spec = pl.BlockSpec((nb, MEM_TOKENS, d), lambda s, i: (s, 0, 0))
    else:
        kv_spec = pl.BlockSpec((None, nb, MEM_TOKENS, XATTN_HEADS, head_dim), lambda s, i: (layer, s, 0, 0, 0))
    q_spec = pl.BlockSpec((rb, d), lambda s, i: (row0 // rb + s * n_q + i, 0))
    (att,) = _pallas(
        functools.partial(_xattn_kernel, tq=tq, nb=nb, head_dim=head_dim, heads_axis=layer is not None),
        grid=(n_seq // nb, n_q),
        in_specs=[q_spec, kv_spec, kv_spec],
        args=[q, mem_k, mem_v],
        out_specs=[q_spec],
        out_shapes=[jax.ShapeDtypeStruct(q.shape, BF16)],
        carried=[att],
        name="cross_attention",
    )
    return att


def kernel(x_prompt, x_sample, mem_prompt, state_s5_re, state_s5_im, state_ret, state_ssm, state_conv, cache_mem_k, cache_mem_v, ffn1_norm, ffn1_w1, ffn1_w3, ffn1_w2, mix_norm, w_in, w_out, s5_lambda_re, s5_lambda_im, s5_b_re, s5_b_im, s5_c_re, s5_c_im, s5_d, s5_log_step, s5_glu_w, s5_glu_b, ret_norm, ssd_conv_w, ssd_conv_b, ssd_dt_bias, ssd_a_log, ssd_d, ssd_norm, xattn_norm, xattn_wq, xattn_wk, xattn_wv, xattn_wo, ffn2_norm, ffn2_w1, ffn2_w3, ffn2_w2, final_norm):
    bp, lp, d = x_prompt.shape
    bs, ls, _ = x_sample.shape
    depth = w_in.shape[0]
    ffn = ffn1_w1.shape[-1]
    n_p, n_s = bp * lp, bs * ls

    def gain(g):
        return g.astype(F32).reshape(-1, 1, g.shape[-1])

    def lane_pad(v):
        return jnp.pad(v.astype(F32), ((0, 0), (0, LANES - v.shape[-1]))).reshape(depth, 1, LANES)

    s5_tabs = s5_tables(s5_lambda_re, s5_lambda_im, s5_b_re, s5_b_im, s5_c_re, s5_c_im, s5_d, s5_log_step,
                        s5_glu_w, s5_glu_b)
    main_cols = w_in.shape[-1] - SSD_HEADS
    w_dt = jnp.pad(w_in[:, :, main_cols:], ((0, 0), (0, 0), (0, LANES - SSD_HEADS)))
    w_in_t = w_in.swapaxes(1, 2)
    ssd_tabs = dict(conv_w=ssd_conv_w.astype(F32), conv_b=gain(ssd_conv_b), dt_bias=lane_pad(ssd_dt_bias),
                    a_neg=lane_pad(-jnp.exp(ssd_a_log.astype(F32))),
                    d=gain(jnp.repeat(ssd_d, SSD_HEAD_DIM, axis=-1)), norm=gain(ssd_norm))
    ret_w = gain(ret_norm)
    cos_p, sin_p = rope_tables(jnp.arange(lp, dtype=jnp.int32))
    cos_s, sin_s = rope_tables(PAST_LEN + jnp.arange(ls, dtype=jnp.int32))
    mem = mem_prompt.reshape(bp * MEM_TOKENS, d).astype(BF16)
    norms = {name: gain(g) for name, g in dict(ffn1=ffn1_norm, mix=mix_norm, xattn=xattn_norm, ffn2=ffn2_norm,
                                                final=final_norm).items()}

    ssm_t = state_ssm.swapaxes(-1, -2)
    conv_t = state_conv.swapaxes(1, 2)
    s5_re_cm = jnp.stack([_to_chunk_major(state_s5_re[l]) for l in range(depth)])
    s5_im_cm = jnp.stack([_to_chunk_major(state_s5_im[l]) for l in range(depth)])
    zeros_s5 = jnp.zeros((S5_LANE_CHUNKS, bp, LANES), F32)
    zeros_ret = jnp.zeros((1, bp) + state_ret.shape[2:], F32)
    zeros_ssm = jnp.zeros((1, bp) + ssm_t.shape[2:], F32)
    zeros_conv = jnp.zeros((1, SSD_CONV - 1, bp, SSD_CONV_DIM), F32)

    x = jnp.concatenate([x_prompt.reshape(n_p, d), x_sample.reshape(n_s, d)], axis=0)
    lists = {name: [] for name in ("p_s5re", "p_s5im", "p_mk", "p_mv", "s_s5re", "s_s5im")}
    p_ret = p_ssm = p_conv = s_ret = s_ssm = s_conv = None

    def ffn_half(x, norm, w1, w3, w2, layer):
        h = rmsnorm(x, norm, layer, BF16)
        a = matmul(h, [w1, w3], layer, n_cols=ffn, out_dtype=BF16)
        return matmul(a, [w2], layer, n_cols=d, out_dtype=F32, res=x, scale=0.5)

    for l in range(depth):
        x = ffn_half(x, norms["ffn1"], ffn1_w1, ffn1_w3, ffn1_w2, l)

        h = rmsnorm(x, norms["mix"], l, BF16)
        proj = matmul(h, [w_in_t], l, n_cols=main_cols, out_dtype=F32, w_transposed=True)
        dt_raw = matmul(h, [w_dt], l, n_cols=LANES, out_dtype=F32)

        mix, hr_p, hi_p = s5_mixer(proj, None, 0, bp, lp, zeros_s5, zeros_s5, s5_tabs, l)
        mix, hr_s, hi_s = s5_mixer(proj, mix, n_p, bs, ls, s5_re_cm[l], s5_im_cm[l], s5_tabs, l)
        mix, p_ret = retention_mixer(proj, mix, 0, bp, lp, zeros_ret, 0, p_ret, depth, cos_p, sin_p, ret_w, l)
        mix, s_ret = retention_mixer(proj, mix, n_p, bs, ls, state_ret, l, s_ret, depth, cos_s, sin_s, ret_w, l)
        mix, p_ssm, p_conv = ssd_mixer(proj, dt_raw, mix, 0, bp, lp, zeros_ssm, zeros_conv, 0, p_ssm, p_conv,
                                       depth, ssd_tabs, l)
        mix, s_ssm, s_conv = ssd_mixer(proj, dt_raw, mix, n_p, bs, ls, ssm_t, conv_t, l, s_ssm, s_conv,
                                       depth, ssd_tabs, l)
        x = matmul(mix, [w_out], l, n_cols=d, out_dtype=F32, res=x)

        h = rmsnorm(x, norms["xattn"], l, BF16)
        q = matmul(h, [xattn_wq], l, n_cols=d, out_dtype=BF16)
        mk = matmul(mem, [xattn_wk], l, n_cols=d, out_dtype=F32)
        mv = matmul(mem, [xattn_wv], l, n_cols=d, out_dtype=F32)
        att = cross_attention(q, None, 0, bp, lp, mk.astype(BF16).reshape(bp, MEM_TOKENS, d),
                              mv.astype(BF16).reshape(bp, MEM_TOKENS, d))
        att = cross_attention(q, att, n_p, bs, ls, cache_mem_k, cache_mem_v, layer=l)
        x = matmul(att, [xattn_wo], l, n_cols=d, out_dtype=F32, res=x)

        x = ffn_half(x, norms["ffn2"], ffn2_w1, ffn2_w3, ffn2_w2, l)

        lists["p_s5re"].append(_from_chunk_major(hr_p))
        lists["p_s5im"].append(_from_chunk_major(hi_p))
        lists["p_mk"].append(mk.reshape(bp, MEM_TOKENS, XATTN_HEADS, d // XATTN_HEADS))
        lists["p_mv"].append(mv.reshape(bp, MEM_TOKENS, XATTN_HEADS, d // XATTN_HEADS))
        lists["s_s5re"].append(_from_chunk_major(hr_s))
        lists["s_s5im"].append(_from_chunk_major(hi_s))

    y_p = rmsnorm(x, norms["final"], 0, F32, 0, n_p)
    y_s = rmsnorm(x, norms["final"], 0, F32, n_p, n_s)
    st = {name: jnp.stack(v) for name, v in lists.items()}
    return (y_p.reshape(bp, lp, d), y_s.reshape(bs, ls, d),
            st["p_s5re"], st["p_s5im"], p_ret, p_ssm.swapaxes(-1, -2), p_conv.swapaxes(1, 2),
            st["p_mk"], st["p_mv"],
            st["s_s5re"], st["s_s5im"], s_ret, s_ssm.swapaxes(-1, -2), s_conv.swapaxes(1, 2))
```

```python
import functools
import math

import jax
import jax.numpy as jnp
from jax import lax
from jax.experimental import pallas as pl
from jax.experimental.pallas import tpu as pltpu

F32 = jnp.float32
BF16 = jnp.bfloat16

EPS = 1e-6
S5_GROUPS = 32
S5_GROUP_CH = 16
S5_STATE = 64
S5_MODES = S5_GROUPS * S5_STATE
RET_HEADS = 4
RET_HEAD_DIM = 128
ROPE_BASE = 10000.0
SSD_HEADS = 16
SSD_HEAD_DIM = 64
SSD_GROUPS = 2
SSD_STATE = 128
SSD_CONV = 4
MEM_TOKENS = 256
XATTN_HEADS = 4
CHUNK = 128
PAST_LEN = 16384

MIX_WIDTH = 2048

LANES = 128
SUBLANES = 8
MIB = 1024 * 1024
VMEM_LIMIT_BYTES = 56 * MIB
MATMUL_VMEM_BUDGET = VMEM_LIMIT_BYTES - 10 * MIB

_NT = (((1,), (1,)), ((), ()))
_TN = (((0,), (0,)), ((), ()))


def _params(n_axes):
    return pltpu.CompilerParams(dimension_semantics=("arbitrary",) * n_axes,
                                vmem_limit_bytes=VMEM_LIMIT_BYTES)


def _pallas(body, *, grid, in_specs, args, out_specs, out_shapes, carried=(), scratch=(), name):
    carried = list(carried) + [None] * (len(out_shapes) - len(carried))
    extra = [(i, c) for i, c in enumerate(carried) if c is not None]
    n_in = len(args)

    def with_carried(*refs):
        body(*refs[:n_in], *refs[n_in + len(extra):])

    return pl.pallas_call(
        with_carried,
        grid=grid,
        in_specs=list(in_specs) + [pl.BlockSpec(memory_space=pl.ANY)] * len(extra),
        out_specs=out_specs,
        out_shape=out_shapes,
        input_output_aliases={n_in + k: i for k, (i, _) in enumerate(extra)},
        scratch_shapes=list(scratch),
        compiler_params=_params(len(grid)),
        name=name,
    )(*args, *[c for _, c in extra])


def _bdot(a, b):
    return jnp.dot(a.astype(BF16), b.astype(BF16), preferred_element_type=F32)


def _split3(x):
    x1 = x.astype(BF16)
    r1 = x - x1.astype(F32)
    x2 = r1.astype(BF16)
    x3 = (r1 - x2.astype(F32)).astype(BF16)
    return x1, x2, x3


def _rmsnorm_kernel(x_ref, g_ref, o_ref):
    x = x_ref[...]
    ms = jnp.mean(x * x, axis=-1, keepdims=True)
    o_ref[...] = ((x * lax.rsqrt(ms + EPS)) * g_ref[...]).astype(o_ref.dtype)


def rmsnorm(x, gains, layer, out_dtype, row0=0, n_rows=None, tm=1024):
    d = x.shape[1]
    m = x.shape[0] if n_rows is None else n_rows
    return pl.pallas_call(
        _rmsnorm_kernel,
        grid=(m // tm,),
        in_specs=[pl.BlockSpec((tm, d), lambda i: (row0 // tm + i, 0)),
                  pl.BlockSpec((None, 1, d), lambda i: (layer, 0, 0))],
        out_specs=pl.BlockSpec((tm, d), lambda i: (i, 0)),
        out_shape=jax.ShapeDtypeStruct((m, d), out_dtype),
        compiler_params=_params(1),
        name="rmsnorm",
    )(x, gains)


def _mm_kernel(*refs, n_w, has_res, scale, w_transposed):
    a_ref = refs[0]
    w_refs = refs[1:1 + n_w]
    res_ref = refs[1 + n_w] if has_res else None
    o_ref = refs[1 + n_w + int(has_res)]
    wb_refs = refs[2 + n_w + int(has_res):]

    @pl.when(pl.program_id(1) == 0)
    def _():
        for w_ref, wb_ref in zip(w_refs, wb_refs):
            wb_ref[...] = w_ref[...].astype(BF16)

    def mm(a, wb_ref):
        if w_transposed:
            return lax.dot_general(a, wb_ref[...], _NT, preferred_element_type=F32)
        return jnp.dot(a, wb_ref[...], preferred_element_type=F32)

    a = a_ref[...]
    acc = mm(a, wb_refs[0])
    if n_w == 2:
        acc = jax.nn.silu(acc) * mm(a, wb_refs[1])
    if has_res:
        acc = res_ref[...] + (acc if scale == 1.0 else scale * acc)
    o_ref[...] = acc.astype(o_ref.dtype)


def _matmul_tiles(m, k, n_cols, n_w, has_res, out_bytes):
    for tn in (1024, 512, 256, 128):
        if n_cols % tn:
            continue
        for tm in (1024, 512, 256):
            if m % tm:
                continue
            need = (2 * tm * k * 2
                    + n_w * (2 * k * tn * 4 + k * tn * 2)
                    + 2 * tm * tn * out_bytes + (2 * tm * tn * 4 if has_res else 0)
                    + n_w * tm * tn * 4)
            if need <= MATMUL_VMEM_BUDGET:
                return tn, tm
    raise ValueError("no matmul tiling fits VMEM")


def matmul(a, ws, layer, *, n_cols, out_dtype, res=None, scale=1.0, w_transposed=False):
    m, k = a.shape
    n_w = len(ws)
    tn, tm = _matmul_tiles(m, k, n_cols, n_w, res is not None, jnp.dtype(out_dtype).itemsize)
    in_specs = [pl.BlockSpec((tm, k), lambda j, i: (i, 0))]
    if w_transposed:
        in_specs += [pl.BlockSpec((None, tn, k), lambda j, i: (layer, j, 0)) for _ in ws]
    else:
        in_specs += [pl.BlockSpec((None, k, tn), lambda j, i: (layer, 0, j)) for _ in ws]
    args = [a, *ws]
    if res is not None:
        in_specs.append(pl.BlockSpec((tm, tn), lambda j, i: (i, j)))
        args.append(res)
    return pl.pallas_call(
        functools.partial(_mm_kernel, n_w=n_w, has_res=res is not None, scale=scale, w_transposed=w_transposed),
        grid=(n_cols // tn, m // tm),
        in_specs=in_specs,
        out_specs=pl.BlockSpec((tm, tn), lambda j, i: (i, j)),
        out_shape=jax.ShapeDtypeStruct((m, n_cols), out_dtype),
        scratch_shapes=[pltpu.VMEM((tn, k) if w_transposed else (k, tn), BF16) for _ in ws],
        compiler_params=_params(2),
        name="matmul",
    )(*args)


def _mm_rows_kernel(a_ref, w_ref, res_ref, g_ref, x_ref, h_ref, wb_ref):
    @pl.when(pl.program_id(0) == 0)
    def _():
        wb_ref[...] = w_ref[...].astype(BF16)

    x = res_ref[...] + jnp.dot(a_ref[...], wb_ref[...], preferred_element_type=F32)
    x_ref[...] = x
    ms = jnp.mean(x * x, axis=-1, keepdims=True)
    h_ref[...] = ((x * lax.rsqrt(ms + EPS)) * g_ref[...]).astype(h_ref.dtype)


def matmul_residual_norm(a, w, layer, res, gains, tm=256):
    m, k = a.shape
    n = res.shape[1]
    row = lambda i: (i, 0)
    return pl.pallas_call(
        _mm_rows_kernel,
        grid=(m // tm,),
        in_specs=[pl.BlockSpec((tm, k), row),
                  pl.BlockSpec((None, k, n), lambda i: (layer, 0, 0), pipeline_mode=pl.Buffered(1)),
                  pl.BlockSpec((tm, n), row),
                  pl.BlockSpec((None, 1, n), lambda i: (layer, 0, 0))],
        out_specs=[pl.BlockSpec((tm, n), row), pl.BlockSpec((tm, n), row)],
        out_shape=[jax.ShapeDtypeStruct((m, n), F32), jax.ShapeDtypeStruct((m, n), BF16)],
        scratch_shapes=[pltpu.VMEM((k, n), BF16)],
        compiler_params=_params(1),
        name="matmul_residual_norm",
    )(a, w, res, gains)


S5_LANE_CHUNKS = S5_MODES // LANES
S5_POWERS = 8
S5_WIDTH = S5_GROUPS * S5_GROUP_CH
S5_SUPER = 4
S5_SUPER_CH = S5_WIDTH // S5_SUPER
S5_SUPER_MODES = S5_MODES // S5_SUPER
S5_SUPER_CHUNKS = S5_SUPER_MODES // LANES
S5_BLOCK_ROWS = 256


def _cmul_add(xr, xi, ar, ai, br, bi, keep=None):
    pr = ar * br - ai * bi
    pi = ar * bi + ai * br
    if keep is not None:
        pr = jnp.where(keep, pr, 0.0)
        pi = jnp.where(keep, pi, 0.0)
    return xr + pr, xi + pi


def _s5_kernel(u_ref, h0r_ref, h0i_ref, pwr_ref, pwi_ref, lnr_ref, lni_ref, bsup_ref, csup_ref, d_ref, gw_ref,
               gb_ref, o_ref, hor_ref, hoi_ref, bur_ref, bui_ref, hs_ref, *, rb, chain):
    ntile = rb // SUBLANES
    u = u_ref[...]
    ub = u.astype(BF16)
    for q in range(S5_SUPER):
        bu = jnp.dot(ub[:, q * S5_SUPER_CH:(q + 1) * S5_SUPER_CH], bsup_ref[q], preferred_element_type=F32)
        for r in range(S5_SUPER_CHUNKS):
            bur_ref[q * S5_SUPER_CHUNKS + r] = bu[:, r * LANES:(r + 1) * LANES]
            bui_ref[q * S5_SUPER_CHUNKS + r] = bu[:, S5_SUPER_MODES + r * LANES:S5_SUPER_MODES + (r + 1) * LANES]

    if chain:
        @pl.when(pl.program_id(1) == 0)
        def _():
            hor_ref[...] = h0r_ref[...]
            hoi_ref[...] = h0i_ref[...]

    step_in_tile = lax.broadcasted_iota(jnp.int32, (ntile, SUBLANES, LANES), 1)
    tile_idx = lax.broadcasted_iota(jnp.int32, (ntile, LANES), 0)
    tile_last = pl.ds(SUBLANES - 1, ntile, stride=SUBLANES)
    tile_steps = SUBLANES.bit_length() - 1

    def chunk(c, carry):
        pr = pwr_ref[c]
        pi = pwi_ref[c]
        xr = bur_ref[c].reshape(ntile, SUBLANES, LANES)
        xi = bui_ref[c].reshape(ntile, SUBLANES, LANES)
        for k in range(tile_steps):
            sh = 1 << k
            xr, xi = _cmul_add(xr, xi, pr[k:k + 1], pi[k:k + 1], pltpu.roll(xr, sh, 1), pltpu.roll(xi, sh, 1),
                               step_in_tile >= sh)
        p8r, p8i = pr[tile_steps:tile_steps + 1], pi[tile_steps:tile_steps + 1]
        if chain:
            bur_ref[c] = xr.reshape(rb, LANES)
            bui_ref[c] = xi.reshape(rb, LANES)
            tr = bur_ref[c, tile_last, :]
            ti = bui_ref[c, tile_last, :]
            hr, hi = hor_ref[c], hoi_ref[c]
            first = tile_idx == 0
            cr, ci = _cmul_add(tr, ti, p8r, p8i, hr, hi, first)
            for j in range(ntile.bit_length() - 1):
                sh = 1 << j
                k = tile_steps + j
                cr, ci = _cmul_add(cr, ci, pr[k:k + 1], pi[k:k + 1], pltpu.roll(cr, sh, 0), pltpu.roll(ci, sh, 0),
                                   tile_idx >= sh)
            prev_r = jnp.where(first, hr, pltpu.roll(cr, 1, 0))
            prev_i = jnp.where(first, hi, pltpu.roll(ci, 1, 0))
            hor_ref[c] = cr[ntile - 1:ntile]
            hoi_ref[c] = ci[ntile - 1:ntile]
        else:
            prev_r, prev_i = h0r_ref[c], h0i_ref[c]
        lr = lnr_ref[c]
        li = lni_ref[c]
        for k in range(ntile):
            rows = slice(k * SUBLANES, (k + 1) * SUBLANES)
            yr, yi = _cmul_add(xr[k], xi[k], lr, li, prev_r[k:k + 1], prev_i[k:k + 1])
            bur_ref[c, rows, :] = yr
            bui_ref[c, rows, :] = yi
        if not chain:
            hor_ref[c] = bur_ref[c, tile_last, :]
            hoi_ref[c] = bui_ref[c, tile_last, :]
        col = pl.multiple_of((c // S5_SUPER_CHUNKS) * 2 * S5_SUPER_MODES + (c % S5_SUPER_CHUNKS) * LANES, LANES)
        hs_ref[:, pl.ds(col, LANES)] = bur_ref[c].astype(BF16)
        hs_ref[:, pl.ds(col + S5_SUPER_MODES, LANES)] = bui_ref[c].astype(BF16)
        return carry

    lax.fori_loop(0, S5_LANE_CHUNKS, chunk, 0)

    y = jnp.concatenate(
        [jnp.dot(hs_ref[:, q * 2 * S5_SUPER_MODES:(q + 1) * 2 * S5_SUPER_MODES], csup_ref[q],
                 preferred_element_type=F32) for q in range(S5_SUPER)], axis=1) + d_ref[...] * u
    y = jax.nn.gelu(y)
    gate = jax.nn.sigmoid(_bdot(y, gw_ref[...]) + gb_ref[...])
    o_ref[...] = (y * gate).astype(o_ref.dtype)


def s5_mixer(proj, mix, row0, n_seq, seq_len, h0r, h0i, tabs, layer):
    chain = seq_len > SUBLANES
    rb = S5_BLOCK_ROWS
    if chain:
        grid = (n_seq, seq_len // rb)
        rows = lambda s, b: (row0 // rb + s * (seq_len // rb) + b, 0)
        h0r = h0r.reshape(S5_LANE_CHUNKS, n_seq, 1, LANES)
        h0i = h0i.reshape(S5_LANE_CHUNKS, n_seq, 1, LANES)
        st_spec = pl.BlockSpec((S5_LANE_CHUNKS, None, 1, LANES), lambda s, b: (0, s, 0, 0))
        st_shape = jax.ShapeDtypeStruct((S5_LANE_CHUNKS, n_seq, 1, LANES), F32)
    else:
        assert seq_len == SUBLANES
        nseg = rb // seq_len
        grid = (n_seq // nseg, 1)
        rows = lambda s, b: (row0 // rb + s, 0)
        st_spec = pl.BlockSpec((S5_LANE_CHUNKS, nseg, LANES), lambda s, b: (0, s, 0))
        st_shape = jax.ShapeDtypeStruct((S5_LANE_CHUNKS, n_seq, LANES), F32)
    const3 = lambda s, b: (layer, 0, 0)
    const4 = lambda s, b: (layer, 0, 0, 0)
    pw_spec = pl.BlockSpec((None, S5_LANE_CHUNKS, S5_POWERS, LANES), const4)
    lin_spec = pl.BlockSpec((None, S5_LANE_CHUNKS, SUBLANES, LANES), const4)
    mix, hr, hi = _pallas(
        functools.partial(_s5_kernel, rb=rb, chain=chain),
        grid=grid,
        in_specs=[pl.BlockSpec((rb, S5_WIDTH), rows), st_spec, st_spec, pw_spec, pw_spec, lin_spec, lin_spec,
                  pl.BlockSpec((None, S5_SUPER, S5_SUPER_CH, 2 * S5_SUPER_MODES), const4),
                  pl.BlockSpec((None, S5_SUPER, 2 * S5_SUPER_MODES, S5_SUPER_CH), const4),
                  pl.BlockSpec((None, 1, S5_WIDTH), const3),
                  pl.BlockSpec((None, S5_WIDTH, S5_WIDTH), const3),
                  pl.BlockSpec((None, 1, S5_WIDTH), const3)],
        args=[proj, h0r, h0i, tabs["pw_re"], tabs["pw_im"], tabs["lin_re"], tabs["lin_im"], tabs["bsup"],
              tabs["csup"], tabs["d"], tabs["glu_w"], tabs["glu_b"]],
        out_specs=[pl.BlockSpec((rb, S5_WIDTH), rows), st_spec, st_spec],
        out_shapes=[jax.ShapeDtypeStruct((proj.shape[0], MIX_WIDTH), BF16), st_shape, st_shape],
        carried=[mix],
        scratch=[pltpu.VMEM((S5_LANE_CHUNKS, rb, LANES), F32),
                 pltpu.VMEM((S5_LANE_CHUNKS, rb, LANES), F32),
                 pltpu.VMEM((rb, 2 * S5_MODES), BF16)],
        name="s5_mixer",
    )
    return mix, hr.reshape(S5_LANE_CHUNKS, n_seq, LANES), hi.reshape(S5_LANE_CHUNKS, n_seq, LANES)


def s5_tables(lam_re, lam_im, b_re, b_im, c_re, c_im, d, log_step, glu_w, glu_b):
    depth = lam_re.shape[0]
    step = jnp.exp(log_step.astype(F32))[..., None]
    lr, li = lam_re.astype(F32), lam_im.astype(F32)
    mag = jnp.exp(lr * step)
    lbr, lbi = mag * jnp.cos(li * step), mag * jnp.sin(li * step)
    den = lr * lr + li * li
    fr = ((lbr - 1.0) * lr + lbi * li) / den
    fi = (lbi * lr - (lbr - 1.0) * li) / den
    bbr = fr[..., None] * b_re - fi[..., None] * b_im
    bbi = fr[..., None] * b_im + fi[..., None] * b_re
    per = S5_GROUPS // S5_SUPER
    eye = jnp.eye(per, dtype=F32)

    def blk_in(t):
        t = t.reshape(depth, S5_SUPER, per, S5_STATE, S5_GROUP_CH)
        return jnp.einsum("dqgpc,gh->dqgchp", t, eye).reshape(depth, S5_SUPER, S5_SUPER_CH, S5_SUPER_MODES)

    def blk_out(t):
        t = t.reshape(depth, S5_SUPER, per, S5_GROUP_CH, S5_STATE)
        return jnp.einsum("dqgcp,gh->dqgphc", t, eye).reshape(depth, S5_SUPER, S5_SUPER_MODES, S5_SUPER_CH)

    bsup = jnp.concatenate([blk_in(bbr), blk_in(bbi)], axis=-1).astype(BF16)
    csup = jnp.concatenate([blk_out(c_re.astype(F32)), blk_out(-c_im.astype(F32))], axis=2).astype(BF16)
    pr, pi = lbr.reshape(depth, S5_MODES), lbi.reshape(depth, S5_MODES)
    prs, pis = [], []
    for _ in range(S5_POWERS):
        prs.append(pr)
        pis.append(pi)
        pr, pi = pr * pr - pi * pi, 2.0 * pr * pi
    lr, li = prs[0], pis[0]
    lrs, lis = [], []
    for _ in range(SUBLANES):
        lrs.append(lr)
        lis.append(li)
        lr, li = lr * prs[0] - li * pis[0], lr * pis[0] + li * prs[0]

    def chunked(ts):
        t = jnp.stack(ts, axis=1).reshape(depth, len(ts), S5_LANE_CHUNKS, LANES)
        return t.transpose(0, 2, 1, 3)

    return dict(pw_re=chunked(prs), pw_im=chunked(pis), lin_re=chunked(lrs), lin_im=chunked(lis),
                bsup=bsup, csup=csup, d=d.astype(F32).reshape(depth, 1, S5_WIDTH), glu_w=glu_w.astype(BF16),
                glu_b=glu_b.astype(F32).reshape(depth, 1, S5_WIDTH))


def _to_chunk_major(h):
    n = h.shape[0]
    return h.reshape(n, S5_LANE_CHUNKS, LANES).transpose(1, 0, 2)


def _from_chunk_major(h):
    n = h.shape[1]
    return h.transpose(1, 0, 2).reshape(n, S5_GROUPS, S5_STATE)


_RET_LOG_GAMMA = tuple(math.log(1.0 - 2.0 ** (-5.0 - h)) for h in range(RET_HEADS))
RET_WIDTH = RET_HEADS * RET_HEAD_DIM


def _ret_kernel(q_ref, k_ref, v_ref, g_ref, cos_ref, sin_ref, nw_ref, s0_ref, o_ref, s_ref, y_ref, *, t, nb):
    @pl.when(pl.program_id(1) == 0)
    def _():
        s_ref[...] = s0_ref[...]

    cos = cos_ref[...]
    sin = sin_ref[...]
    ii = lax.broadcasted_iota(jnp.int32, (t, t), 0)
    jj = lax.broadcasted_iota(jnp.int32, (t, t), 1)
    ti = lax.broadcasted_iota(jnp.int32, (t, 1), 0).astype(F32)
    half = RET_HEAD_DIM // 2
    for h in range(RET_HEADS):
        lg = _RET_LOG_GAMMA[h]
        cols = slice(h * RET_HEAD_DIM, (h + 1) * RET_HEAD_DIM)
        decay = jnp.exp(jnp.where(ii >= jj, (ii - jj).astype(F32) * lg, -jnp.inf))
        grow = jnp.exp((ti + 1.0) * lg)
        tail = jnp.exp((t - 1.0 - ti) * lg)
        for n in range(nb):
            rows = slice(n * t, (n + 1) * t)
            q = q_ref[rows, cols]
            k = k_ref[rows, cols]
            v = v_ref[rows, cols].astype(BF16)
            q = q * cos + pltpu.roll(q, half, 1) * sin
            k = (k * cos + pltpu.roll(k, half, 1) * sin) * (RET_HEAD_DIM ** -0.5)
            s_prev = s_ref[n, h]
            scores = lax.dot_general(q.astype(BF16), k.astype(BF16), _NT, preferred_element_type=F32) * decay
            y = _bdot(scores, v) + _bdot(q * grow, s_prev)
            s_ref[n, h] = math.exp(t * lg) * s_prev + lax.dot_general(
                (k * tail).astype(BF16), v, _TN, preferred_element_type=F32)
            y = y * lax.rsqrt(jnp.mean(y * y, axis=-1, keepdims=True) + EPS)
            y = y * nw_ref[:, cols]
            y_ref[rows, cols] = jax.nn.silu(g_ref[rows, cols]) * y
    o_ref[...] = y_ref[...].astype(o_ref.dtype)


def retention_mixer(proj, mix, row0, n_seq, seq_len, s0, s0_layer, s_out, depth, cos2, sin2, norm_w, layer):
    t = math.gcd(seq_len, CHUNK)
    n_chunks = seq_len // t
    nb = 1 if n_chunks > 1 else 8
    rb = nb * t

    def col(cb):
        return pl.BlockSpec((rb, RET_WIDTH), lambda s, c: (row0 // rb + s * n_chunks + c, cb))

    st_block = (None, nb, RET_HEADS, RET_HEAD_DIM, RET_HEAD_DIM)
    tab_spec = pl.BlockSpec((t, RET_HEAD_DIM), lambda s, c: (c, 0))
    return _pallas(
        functools.partial(_ret_kernel, t=t, nb=nb),
        grid=(n_seq // nb, n_chunks),
        in_specs=[col(1), col(2), col(3), col(4), tab_spec, tab_spec,
                  pl.BlockSpec((None, 1, RET_WIDTH), lambda s, c: (layer, 0, 0)),
                  pl.BlockSpec(st_block, lambda s, c: (s0_layer, s, 0, 0, 0))],
        args=[proj, proj, proj, proj, cos2, sin2, norm_w, s0],
        out_specs=[col(1), pl.BlockSpec(st_block, lambda s, c: (layer, s, 0, 0, 0))],
        out_shapes=[jax.ShapeDtypeStruct((proj.shape[0], MIX_WIDTH), BF16),
                    jax.ShapeDtypeStruct((depth,) + s0.shape[1:], F32)],
        carried=[mix, s_out],
        scratch=[pltpu.VMEM((rb, RET_WIDTH), F32)],
        name="retention_mixer",
    )


def rope_tables(pos):
    half = RET_HEAD_DIM // 2
    inv = ROPE_BASE ** (-jnp.arange(half, dtype=F32) / half)
    ang = pos.astype(F32)[:, None] * inv[None, :]
    cos, sin = jnp.cos(ang), jnp.sin(ang)
    return jnp.concatenate([cos, cos], axis=-1), jnp.concatenate([-sin, sin], axis=-1)


SSD_WIDTH = SSD_HEADS * SSD_HEAD_DIM
SSD_BC = SSD_GROUPS * SSD_STATE
SSD_CONV_DIM = SSD_WIDTH + 2 * SSD_BC
_SSD_PREV = SUBLANES
_COL_BLOCK = 512
_SSD_Z_COL = 2560
_SSD_XBC_COL = _SSD_Z_COL + SSD_WIDTH


def _ssd_kernel(z0_ref, z1_ref, x0_ref, x1_ref, x2_ref, dt_ref, cs_ref, cw_ref, cb_ref, dtb_ref, aneg_ref, dsk_ref,
                nw_ref, s0_ref, o_ref, s_ref, co_ref, ext_ref, xbc_ref, y_ref, xdt_ref, xtl_ref, fs_ref, *,
                t, nb, conv_rows):
    keep = SSD_CONV - 1
    rep = SSD_HEADS // SSD_GROUPS
    grp = rep * SSD_HEAD_DIM

    def conv_row(n):
        return pl.ds(n, 1) if conv_rows == nb else pl.ds(pl.program_id(0) * nb + n, 1)

    @pl.when(pl.program_id(1) == 0)
    def _():
        s_ref[...] = s0_ref[...]
        for n in range(nb):
            for r in range(keep):
                ext_ref[n, _SSD_PREV - keep + r:_SSD_PREV - keep + r + 1, :] = cs_ref[r, conv_row(n), :]

    rb = nb * t
    ii = lax.broadcasted_iota(jnp.int32, (rb, rb), 0)
    jj = lax.broadcasted_iota(jnp.int32, (rb, rb), 1)
    same_seq = (ii // t) == (jj // t)
    causal = jnp.logical_and(same_seq, ii >= jj)
    tri = causal.astype(BF16)
    seq_ones = same_seq.astype(BF16)
    cw = cw_ref[...]
    for n in range(nb):
        rows = slice(n * t, (n + 1) * t)
        ext_ref[n, _SSD_PREV:_SSD_PREV + t, 0:_COL_BLOCK] = x0_ref[rows, :]
        ext_ref[n, _SSD_PREV:_SSD_PREV + t, _COL_BLOCK:2 * _COL_BLOCK] = x1_ref[rows, :]
        ext_ref[n, _SSD_PREV:_SSD_PREV + t, 2 * _COL_BLOCK:] = x2_ref[rows, :]
        conv = cb_ref[...] + cw[keep:keep + 1] * ext_ref[n, _SSD_PREV:_SSD_PREV + t, :]
        for back in range(1, SSD_CONV):
            conv = conv + cw[keep - back:keep - back + 1] * ext_ref[n, pl.ds(_SSD_PREV - back, t), :]
        tail_rows = ext_ref[n, _SSD_PREV + t - keep:_SSD_PREV + t, :]
        for r in range(keep):
            co_ref[r, conv_row(n), :] = tail_rows[r:r + 1]
        ext_ref[n, _SSD_PREV - keep:_SSD_PREV, :] = tail_rows
        xbc_ref[rows, :] = jax.nn.silu(conv)

    dt = jax.nn.softplus(dt_ref[...] + dtb_ref[...])
    la = dt * aneg_ref[...]
    pieces = _split3(la)
    cum = sum(jnp.dot(tri, p, preferred_element_type=F32) for p in pieces)
    tot = sum(jnp.dot(seq_ones, p, preferred_element_type=F32) for p in pieces)
    cum_t = cum.T
    e_last = jnp.exp(tot)

    spread = (lax.broadcasted_iota(jnp.int32, (LANES, SSD_WIDTH), 0)
              == lax.broadcasted_iota(jnp.int32, (LANES, SSD_WIDTH), 1) // SSD_HEAD_DIM).astype(BF16)

    def per_column(v):
        return sum(jnp.dot(p, spread, preferred_element_type=F32) for p in _split3(v))

    xh = xbc_ref[:, :SSD_WIDTH]
    xdt_ref[...] = xh * per_column(dt)
    xtl_ref[...] = xh * per_column(dt * jnp.exp(tot - cum))
    e_cum_cols = per_column(jnp.exp(cum))
    first_of_pair = lax.broadcasted_iota(jnp.int32, (1, 2 * SSD_HEAD_DIM), 1) < SSD_HEAD_DIM
    for g in range(SSD_GROUPS):
        gc = slice(g * grp, (g + 1) * grp)
        bg = xbc_ref[:, SSD_WIDTH + g * SSD_STATE:SSD_WIDTH + (g + 1) * SSD_STATE]
        cg = xbc_ref[:, SSD_WIDTH + SSD_BC + g * SSD_STATE:SSD_WIDTH + SSD_BC + (g + 1) * SSD_STATE]
        gram = lax.dot_general(cg.astype(BF16), bg.astype(BF16), _NT, preferred_element_type=F32)
        for n in range(nb):
            rows = slice(n * t, (n + 1) * t)
            s_grp = s_ref[n, g * rep:(g + 1) * rep].reshape(grp, SSD_STATE)
            fs_ref[rows, gc] = lax.dot_general(cg[rows].astype(BF16), s_grp.astype(BF16), _NT,
                                               preferred_element_type=F32)
            s_add = lax.dot_general(xtl_ref[rows, gc].astype(BF16), bg[rows].astype(BF16), _TN,
                                    preferred_element_type=F32)
            for hh in range(rep):
                h = g * rep + hh
                s_ref[n, h] = (e_last[n * t:n * t + 1, h:h + 1] * s_ref[n, h]
                               + s_add[hh * SSD_HEAD_DIM:(hh + 1) * SSD_HEAD_DIM, :])
        for pair in range(rep // 2):
            h = g * rep + 2 * pair
            pc = slice(h * SSD_HEAD_DIM, (h + 2) * SSD_HEAD_DIM)
            xp = xdt_ref[:, pc]
            intra = 0.0
            for k in range(2):
                decay = jnp.exp(jnp.where(causal, cum[:, h + k:h + k + 1] - cum_t[h + k:h + k + 1, :], -jnp.inf))
                own = first_of_pair if k == 0 else jnp.logical_not(first_of_pair)
                intra = intra + _bdot(gram * decay, jnp.where(own, xp, 0.0))
            y_ref[:, pc] = intra
    y = y_ref[...] + e_cum_cols * fs_ref[...] + xh * dsk_ref[...]
    y = jnp.concatenate([y[:, :_COL_BLOCK] * jax.nn.silu(z0_ref[...]),
                         y[:, _COL_BLOCK:] * jax.nn.silu(z1_ref[...])], axis=1)
    y = y * lax.rsqrt(jnp.mean(y * y, axis=-1, keepdims=True) + EPS)
    o_ref[...] = (y * nw_ref[...]).astype(o_ref.dtype)


def ssd_mixer(proj, dt_raw, mix, row0, n_seq, seq_len, s0, conv0, st_layer, s_out, conv_out, depth, tabs, layer):
    t = math.gcd(seq_len, CHUNK)
    n_chunks = seq_len // t
    nb = 1 if n_chunks > 1 else SUBLANES
    rb = nb * t
    conv_rows = nb if nb % SUBLANES == 0 else n_seq

    def col(cb, width=_COL_BLOCK):
        return pl.BlockSpec((rb, width), lambda s, c: (row0 // rb + s * n_chunks + c, cb))

    def conv_idx(lyr):
        return lambda s, c: (lyr, 0, s if conv_rows == nb else 0, 0)

    z_cb = _SSD_Z_COL // _COL_BLOCK
    x_cb = _SSD_XBC_COL // _COL_BLOCK
    const3 = lambda s, c: (layer, 0, 0)
    st_block = (None, nb, SSD_HEADS, SSD_HEAD_DIM, SSD_STATE)
    cs_block = (None, SSD_CONV - 1, conv_rows, SSD_CONV_DIM)
    return _pallas(
        functools.partial(_ssd_kernel, t=t, nb=nb, conv_rows=conv_rows),
        grid=(n_seq // nb, n_chunks),
        in_specs=[col(z_cb), col(z_cb + 1), col(x_cb), col(x_cb + 1), col(x_cb + 2), col(0, LANES),
                  pl.BlockSpec(cs_block, conv_idx(st_layer)),
                  pl.BlockSpec((None, SSD_CONV, SSD_CONV_DIM), const3),
                  pl.BlockSpec((None, 1, SSD_CONV_DIM), const3),
                  pl.BlockSpec((None, 1, LANES), const3),
                  pl.BlockSpec((None, 1, LANES), const3),
                  pl.BlockSpec((None, 1, SSD_WIDTH), const3),
                  pl.BlockSpec((None, 1, SSD_WIDTH), const3),
                  pl.BlockSpec(st_block, lambda s, c: (st_layer, s, 0, 0, 0))],
        args=[proj, proj, proj, proj, proj, dt_raw, conv0, tabs["conv_w"], tabs["conv_b"], tabs["dt_bias"],
              tabs["a_neg"], tabs["d"], tabs["norm"], s0],
        out_specs=[col(1, SSD_WIDTH),
                   pl.BlockSpec(st_block, lambda s, c: (layer, s, 0, 0, 0)),
                   pl.BlockSpec(cs_block, conv_idx(layer))],
        out_shapes=[jax.ShapeDtypeStruct((proj.shape[0], MIX_WIDTH), BF16),
                    jax.ShapeDtypeStruct((depth,) + s0.shape[1:], F32),
                    jax.ShapeDtypeStruct((depth,) + conv0.shape[1:], F32)],
        carried=[mix, s_out, conv_out],
        scratch=[pltpu.VMEM((nb, _SSD_PREV + t, SSD_CONV_DIM), F32),
                 pltpu.VMEM((rb, SSD_CONV_DIM), F32)] + [pltpu.VMEM((rb, SSD_WIDTH), F32)] * 4,
        name="ssd_mixer",
    )


def _softmax(s):
    s = s - jnp.max(s, axis=-1, keepdims=True)
    p = jnp.exp(s)
    return p / jnp.sum(p, axis=-1, keepdims=True)


def _xattn_kernel(q_ref, k_ref, v_ref, o_ref, *, tq, nb, head_dim, heads_axis):
    if not heads_axis:
        for h in range(XATTN_HEADS):
            cols = slice(h * head_dim, (h + 1) * head_dim)
            q = q_ref[:, cols]
            pieces = []
            for n in range(nb):
                s = lax.dot_general(q, k_ref[n, :, cols].astype(BF16), _NT,
                                    preferred_element_type=F32)[n * tq:(n + 1) * tq]
                pieces.append(_bdot(_softmax(s * (head_dim ** -0.5)), v_ref[n, :, cols]))
            att = pieces[0] if nb == 1 else jnp.concatenate(pieces, axis=0)
            o_ref[:, cols] = att.astype(o_ref.dtype)
        return

    flat = MEM_TOKENS * XATTN_HEADS
    rows = XATTN_HEADS * tq
    row_head = lax.broadcasted_iota(jnp.int32, (rows, flat), 0) // tq
    col_head = lax.broadcasted_iota(jnp.int32, (rows, flat), 1) % XATTN_HEADS
    same_head = row_head == col_head
    q_all = q_ref[...].astype(F32)
    outs = []
    for n in range(nb):
        q = jnp.concatenate([q_all[n * tq:(n + 1) * tq, h * head_dim:(h + 1) * head_dim]
                             for h in range(XATTN_HEADS)], axis=0)
        k = k_ref[n].reshape(flat, head_dim)
        v = v_ref[n].reshape(flat, head_dim)
        s = lax.dot_general(q.astype(BF16), k.astype(BF16), _NT, preferred_element_type=F32)
        s = jnp.where(same_head, s * (head_dim ** -0.5), -jnp.inf)
        outs.append(_bdot(_softmax(s), v))
    for h in range(XATTN_HEADS):
        att = jnp.concatenate([o[h * tq:(h + 1) * tq] for o in outs], axis=0)
        o_ref[:, h * head_dim:(h + 1) * head_dim] = att.astype(o_ref.dtype)


def cross_attention(q, att, row0, n_seq, seq_len, mem_k, mem_v, layer=None):
    d = q.shape[-1]
    head_dim = d // XATTN_HEADS
    if seq_len > SUBLANES:
        tq, nb = 512, 1
    else:
        tq, nb = seq_len, 2
    n_q = seq_len // tq
    rb = nb * tq
    if layer is None:
        kv_spec = pl.BlockSpec((nb, MEM_TOKENS, d), lambda s, i: (s, 0, 0))
    else:
        kv_spec = pl.BlockSpec((None, nb, MEM_TOKENS, XATTN_HEADS, head_dim), lambda s, i: (layer, s, 0, 0, 0))
    q_spec = pl.BlockSpec((rb, d), lambda s, i: (row0 // rb + s * n_q + i, 0))
    (att,) = _pallas(
        functools.partial(_xattn_kernel, tq=tq, nb=nb, head_dim=head_dim, heads_axis=layer is not None),
        grid=(n_seq // nb, n_q),
        in_specs=[q_spec, kv_spec, kv_spec],
        args=[q, mem_k, mem_v],
        out_specs=[q_spec],
        out_shapes=[jax.ShapeDtypeStruct(q.shape, BF16)],
        carried=[att],
        name="cross_attention",
    )
    return att


def kernel(x_prompt, x_sample, mem_prompt, state_s5_re, state_s5_im, state_ret, state_ssm, state_conv, cache_mem_k, cache_mem_v, ffn1_norm, ffn1_w1, ffn1_w3, ffn1_w2, mix_norm, w_in, w_out, s5_lambda_re, s5_lambda_im, s5_b_re, s5_b_im, s5_c_re, s5_c_im, s5_d, s5_log_step, s5_glu_w, s5_glu_b, ret_norm, ssd_conv_w, ssd_conv_b, ssd_dt_bias, ssd_a_log, ssd_d, ssd_norm, xattn_norm, xattn_wq, xattn_wk, xattn_wv, xattn_wo, ffn2_norm, ffn2_w1, ffn2_w3, ffn2_w2, final_norm):
    bp, lp, d = x_prompt.shape
    bs, ls, _ = x_sample.shape
    depth = w_in.shape[0]
    ffn = ffn1_w1.shape[-1]
    n_p, n_s = bp * lp, bs * ls

    def gain(g):
        return g.astype(F32).reshape(-1, 1, g.shape[-1])

    def lane_pad(v):
        return jnp.pad(v.astype(F32), ((0, 0), (0, LANES - v.shape[-1]))).reshape(depth, 1, LANES)

    s5_tabs = s5_tables(s5_lambda_re, s5_lambda_im, s5_b_re, s5_b_im, s5_c_re, s5_c_im, s5_d, s5_log_step,
                        s5_glu_w, s5_glu_b)
    main_cols = w_in.shape[-1] - SSD_HEADS
    w_dt = jnp.pad(w_in[:, :, main_cols:], ((0, 0), (0, 0), (0, LANES - SSD_HEADS)))
    w_in_t = w_in.swapaxes(1, 2)
    ssd_tabs = dict(conv_w=ssd_conv_w.astype(F32), conv_b=gain(ssd_conv_b), dt_bias=lane_pad(ssd_dt_bias),
                    a_neg=lane_pad(-jnp.exp(ssd_a_log.astype(F32))),
                    d=gain(jnp.repeat(ssd_d, SSD_HEAD_DIM, axis=-1)), norm=gain(ssd_norm))
    ret_w = gain(ret_norm)
    cos_p, sin_p = rope_tables(jnp.arange(lp, dtype=jnp.int32))
    cos_s, sin_s = rope_tables(PAST_LEN + jnp.arange(ls, dtype=jnp.int32))
    mem = mem_prompt.reshape(bp * MEM_TOKENS, d).astype(BF16)
    norms = {name: gain(g) for name, g in dict(ffn1=ffn1_norm, mix=mix_norm, xattn=xattn_norm, ffn2=ffn2_norm,
                                                final=final_norm).items()}

    ssm_t = state_ssm.swapaxes(-1, -2)
    conv_t = state_conv.swapaxes(1, 2)
    s5_re_cm = jnp.stack([_to_chunk_major(state_s5_re[l]) for l in range(depth)])
    s5_im_cm = jnp.stack([_to_chunk_major(state_s5_im[l]) for l in range(depth)])
    zeros_s5 = jnp.zeros((S5_LANE_CHUNKS, bp, LANES), F32)
    zeros_ret = jnp.zeros((1, bp) + state_ret.shape[2:], F32)
    zeros_ssm = jnp.zeros((1, bp) + ssm_t.shape[2:], F32)
    zeros_conv = jnp.zeros((1, SSD_CONV - 1, bp, SSD_CONV_DIM), F32)

    x = jnp.concatenate([x_prompt.reshape(n_p, d), x_sample.reshape(n_s, d)], axis=0)
    lists = {name: [] for name in ("p_s5re", "p_s5im", "p_mk", "p_mv", "s_s5re", "s_s5im")}
    p_ret = p_ssm = p_conv = s_ret = s_ssm = s_conv = None

    def ffn_half(x, h, w1, w3, w2, layer):
        a = matmul(h, [w1, w3], layer, n_cols=ffn, out_dtype=BF16)
        return matmul(a, [w2], layer, n_cols=d, out_dtype=F32, res=x, scale=0.5)

    for l in range(depth):
        x = ffn_half(x, rmsnorm(x, norms["ffn1"], l, BF16), ffn1_w1, ffn1_w3, ffn1_w2, l)

        h = rmsnorm(x, norms["mix"], l, BF16)
        proj = matmul(h, [w_in_t], l, n_cols=main_cols, out_dtype=F32, w_transposed=True)
        dt_raw = matmul(h, [w_dt], l, n_cols=LANES, out_dtype=F32)

        mix, hr_p, hi_p = s5_mixer(proj, None, 0, bp, lp, zeros_s5, zeros_s5, s5_tabs, l)
        mix, hr_s, hi_s = s5_mixer(proj, mix, n_p, bs, ls, s5_re_cm[l], s5_im_cm[l], s5_tabs, l)
        mix, p_ret = retention_mixer(proj, mix, 0, bp, lp, zeros_ret, 0, p_ret, depth, cos_p, sin_p, ret_w, l)
        mix, s_ret = retention_mixer(proj, mix, n_p, bs, ls, state_ret, l, s_ret, depth, cos_s, sin_s, ret_w, l)
        mix, p_ssm, p_conv = ssd_mixer(proj, dt_raw, mix, 0, bp, lp, zeros_ssm, zeros_conv, 0, p_ssm, p_conv,
                                       depth, ssd_tabs, l)
        mix, s_ssm, s_conv = ssd_mixer(proj, dt_raw, mix, n_p, bs, ls, ssm_t, conv_t, l, s_ssm, s_conv,
                                       depth, ssd_tabs, l)
        x, h = matmul_residual_norm(mix, w_out, l, x, norms["xattn"])
        q = matmul(h, [xattn_wq], l, n_cols=d, out_dtype=BF16)
        mk = matmul(mem, [xattn_wk], l, n_cols=d, out_dtype=F32)
        mv = matmul(mem, [xattn_wv], l, n_cols=d, out_dtype=F32)
        att = cross_attention(q, None, 0, bp, lp, mk.astype(BF16).reshape(bp, MEM_TOKENS, d),
                              mv.astype(BF16).reshape(bp, MEM_TOKENS, d))
        att = cross_attention(q, att, n_p, bs, ls, cache_mem_k, cache_mem_v, layer=l)
        x, h = matmul_residual_norm(att, xattn_wo, l, x, norms["ffn2"])
        x = ffn_half(x, h, ffn2_w1, ffn2_w3, ffn2_w2, l)

        lists["p_s5re"].append(_from_chunk_major(hr_p))
        lists["p_s5im"].append(_from_chunk_major(hi_p))
        lists["p_mk"].append(mk.reshape(bp, MEM_TOKENS, XATTN_HEADS, d // XATTN_HEADS))
        lists["p_mv"].append(mv.reshape(bp, MEM_TOKENS, XATTN_HEADS, d // XATTN_HEADS))
        lists["s_s5re"].append(_from_chunk_major(hr_s))
        lists["s_s5im"].append(_from_chunk_major(hi_s))

    y_p = rmsnorm(x, norms["final"], 0, F32, 0, n_p)
    y_s = rmsnorm(x, norms["final"], 0, F32, n_p, n_s)
    st = {name: jnp.stack(v) for name, v in lists.items()}
    return (y_p.reshape(bp, lp, d), y_s.reshape(bs, ls, d),
            st["p_s5re"], st["p_s5im"], p_ret, p_ssm.swapaxes(-1, -2), p_conv.swapaxes(1, 2),
            st["p_mk"], st["p_mv"],
            st["s_s5re"], st["s_s5im"], s_ret, s_ssm.swapaxes(-1, -2), s_conv.swapaxes(1, 2))
```

```python
import functools
import math

import jax
import jax.numpy as jnp
from jax import lax
from jax.experimental import pallas as pl
from jax.experimental.pallas import tpu as pltpu

F32 = jnp.float32
BF16 = jnp.bfloat16

EPS = 1e-6
S5_GROUPS = 32
S5_GROUP_CH = 16
S5_STATE = 64
S5_MODES = S5_GROUPS * S5_STATE
RET_HEADS = 4
RET_HEAD_DIM = 128
ROPE_BASE = 10000.0
SSD_HEADS = 16
SSD_HEAD_DIM = 64
SSD_GROUPS = 2
SSD_STATE = 128
SSD_CONV = 4
MEM_TOKENS = 256
XATTN_HEADS = 4
CHUNK = 128
PAST_LEN = 16384

MIX_WIDTH = 2048

LANES = 128
SUBLANES = 8
MIB = 1024 * 1024
VMEM_LIMIT_BYTES = 60 * MIB
MATMUL_VMEM_BUDGET = VMEM_LIMIT_BYTES - 10 * MIB

_NT = (((1,), (1,)), ((), ()))
_TN = (((0,), (0,)), ((), ()))


def _params(n_axes):
    return pltpu.CompilerParams(dimension_semantics=("arbitrary",) * n_axes,
                                vmem_limit_bytes=VMEM_LIMIT_BYTES)


def _pallas(body, *, grid, in_specs, args, out_specs, out_shapes, carried=(), scratch=(), name):
    carried = list(carried) + [None] * (len(out_shapes) - len(carried))
    extra = [(i, c) for i, c in enumerate(carried) if c is not None]
    n_in = len(args)

    def with_carried(*refs):
        body(*refs[:n_in], *refs[n_in + len(extra):])

    return pl.pallas_call(
        with_carried,
        grid=grid,
        in_specs=list(in_specs) + [pl.BlockSpec(memory_space=pl.ANY)] * len(extra),
        out_specs=out_specs,
        out_shape=out_shapes,
        input_output_aliases={n_in + k: i for k, (i, _) in enumerate(extra)},
        scratch_shapes=list(scratch),
        compiler_params=_params(len(grid)),
        name=name,
    )(*args, *[c for _, c in extra])


def _bdot(a, b):
    return jnp.dot(a.astype(BF16), b.astype(BF16), preferred_element_type=F32)


def _split3(x):
    x1 = x.astype(BF16)
    r1 = x - x1.astype(F32)
    x2 = r1.astype(BF16)
    x3 = (r1 - x2.astype(F32)).astype(BF16)
    return x1, x2, x3


def _rmsnorm_kernel(x_ref, g_ref, o_ref):
    x = x_ref[...]
    ms = jnp.mean(x * x, axis=-1, keepdims=True)
    o_ref[...] = ((x * lax.rsqrt(ms + EPS)) * g_ref[...]).astype(o_ref.dtype)


def rmsnorm(x, gains, layer, out_dtype, row0=0, n_rows=None, tm=1024):
    d = x.shape[1]
    m = x.shape[0] if n_rows is None else n_rows
    return pl.pallas_call(
        _rmsnorm_kernel,
        grid=(m // tm,),
        in_specs=[pl.BlockSpec((tm, d), lambda i: (row0 // tm + i, 0)),
                  pl.BlockSpec((None, 1, d), lambda i: (layer, 0, 0))],
        out_specs=pl.BlockSpec((tm, d), lambda i: (i, 0)),
        out_shape=jax.ShapeDtypeStruct((m, d), out_dtype),
        compiler_params=_params(1),
        name="rmsnorm",
    )(x, gains)


def _mm_kernel(*refs, n_w, has_res, scale, w_transposed):
    a_ref = refs[0]
    w_refs = refs[1:1 + n_w]
    res_ref = refs[1 + n_w] if has_res else None
    o_ref = refs[1 + n_w + int(has_res)]
    wb_refs = refs[2 + n_w + int(has_res):]

    @pl.when(pl.program_id(1) == 0)
    def _():
        for w_ref, wb_ref in zip(w_refs, wb_refs):
            wb_ref[...] = w_ref[...].astype(BF16)

    def mm(a, wb_ref):
        if w_transposed:
            return lax.dot_general(a, wb_ref[...], _NT, preferred_element_type=F32)
        return jnp.dot(a, wb_ref[...], preferred_element_type=F32)

    a = a_ref[...]
    acc = mm(a, wb_refs[0])
    if n_w == 2:
        acc = jax.nn.silu(acc) * mm(a, wb_refs[1])
    if has_res:
        acc = res_ref[...] + (acc if scale == 1.0 else scale * acc)
    o_ref[...] = acc.astype(o_ref.dtype)


def _matmul_tiles(m, k, n_cols, n_w, has_res, out_bytes):
    for tn in (1024, 512, 256, 128):
        if n_cols % tn:
            continue
        for tm in (1024, 512, 256):
            if m % tm:
                continue
            need = (2 * tm * k * 2
                    + n_w * (2 * k * tn * 4 + k * tn * 2)
                    + 2 * tm * tn * out_bytes + (2 * tm * tn * 4 if has_res else 0)
                    + n_w * tm * tn * 4)
            if need <= MATMUL_VMEM_BUDGET:
                return tn, tm
    raise ValueError("no matmul tiling fits VMEM")


def matmul(a, ws, layer, *, n_cols, out_dtype, res=None, scale=1.0, w_transposed=False):
    m, k = a.shape
    n_w = len(ws)
    tn, tm = _matmul_tiles(m, k, n_cols, n_w, res is not None, jnp.dtype(out_dtype).itemsize)
    in_specs = [pl.BlockSpec((tm, k), lambda j, i: (i, 0))]
    if w_transposed:
        in_specs += [pl.BlockSpec((None, tn, k), lambda j, i: (layer, j, 0)) for _ in ws]
    else:
        in_specs += [pl.BlockSpec((None, k, tn), lambda j, i: (layer, 0, j)) for _ in ws]
    args = [a, *ws]
    if res is not None:
        in_specs.append(pl.BlockSpec((tm, tn), lambda j, i: (i, j)))
        args.append(res)
    return pl.pallas_call(
        functools.partial(_mm_kernel, n_w=n_w, has_res=res is not None, scale=scale, w_transposed=w_transposed),
        grid=(n_cols // tn, m // tm),
        in_specs=in_specs,
        out_specs=pl.BlockSpec((tm, tn), lambda j, i: (i, j)),
        out_shape=jax.ShapeDtypeStruct((m, n_cols), out_dtype),
        scratch_shapes=[pltpu.VMEM((tn, k) if w_transposed else (k, tn), BF16) for _ in ws],
        compiler_params=_params(2),
        name="matmul",
    )(*args)


_ROWS_K_CHUNK = 512


def _mm_rows_kernel(a_ref, w_ref, res_ref, g_ref, x_ref, h_ref, wb_ref, *, n_pro, scale):
    s = pl.program_id(0)

    @pl.when(s < n_pro)
    def _():
        start = pl.multiple_of(s * _ROWS_K_CHUNK, _ROWS_K_CHUNK)
        wb_ref[pl.ds(start, _ROWS_K_CHUNK), :] = w_ref[...].astype(BF16)

    @pl.when(s >= n_pro)
    def _():
        acc = jnp.dot(a_ref[...], wb_ref[...], preferred_element_type=F32)
        x = res_ref[...] + (acc if scale == 1.0 else scale * acc)
        x_ref[...] = x
        ms = jnp.mean(x * x, axis=-1, keepdims=True)
        h_ref[...] = ((x * lax.rsqrt(ms + EPS)) * g_ref[...]).astype(h_ref.dtype)


def matmul_residual_norm(a, w, layer, res, gains, gain_layer, scale=1.0):
    m, k = a.shape
    n = res.shape[1]
    n_pro = k // _ROWS_K_CHUNK
    fixed = k * n * 2 + 2 * _ROWS_K_CHUNK * n * 4
    for tm in (512, 256, 128):
        per_tile = 2 * tm * k * 2 + 2 * tm * n * (4 + 4 + 2) + tm * n * 4
        if m % tm == 0 and fixed + per_tile <= MATMUL_VMEM_BUDGET:
            break
    else:
        raise ValueError("no row tile fits VMEM")
    row = lambda s: (jnp.maximum(s - n_pro, 0), 0)
    return pl.pallas_call(
        functools.partial(_mm_rows_kernel, n_pro=n_pro, scale=scale),
        grid=(n_pro + m // tm,),
        in_specs=[pl.BlockSpec((tm, k), row),
                  pl.BlockSpec((None, _ROWS_K_CHUNK, n), lambda s: (layer, jnp.minimum(s, n_pro - 1), 0)),
                  pl.BlockSpec((tm, n), row),
                  pl.BlockSpec((None, 1, n), lambda s: (gain_layer, 0, 0))],
        out_specs=[pl.BlockSpec((tm, n), row), pl.BlockSpec((tm, n), row)],
        out_shape=[jax.ShapeDtypeStruct((m, n), F32), jax.ShapeDtypeStruct((m, n), BF16)],
        scratch_shapes=[pltpu.VMEM((k, n), BF16)],
        compiler_params=_params(1),
        name="matmul_residual_norm",
    )(a, w, res, gains)


S5_LANE_CHUNKS = S5_MODES // LANES
S5_POWERS = 8
S5_WIDTH = S5_GROUPS * S5_GROUP_CH
S5_SUPER = 4
S5_SUPER_CH = S5_WIDTH // S5_SUPER
S5_SUPER_MODES = S5_MODES // S5_SUPER
S5_SUPER_CHUNKS = S5_SUPER_MODES // LANES
S5_BLOCK_ROWS = 256


def _cmul_add(xr, xi, ar, ai, br, bi, keep=None):
    pr = ar * br - ai * bi
    pi = ar * bi + ai * br
    if keep is not None:
        pr = jnp.where(keep, pr, 0.0)
        pi = jnp.where(keep, pi, 0.0)
    return xr + pr, xi + pi


def _s5_kernel(u_ref, h0r_ref, h0i_ref, pwr_ref, pwi_ref, lnr_ref, lni_ref, bsup_ref, csup_ref, d_ref, gw_ref,
               gb_ref, o_ref, hor_ref, hoi_ref, bur_ref, bui_ref, hs_ref, *, rb, chain):
    ntile = rb // SUBLANES
    u = u_ref[...]
    ub = u.astype(BF16)
    for q in range(S5_SUPER):
        bu = jnp.dot(ub[:, q * S5_SUPER_CH:(q + 1) * S5_SUPER_CH], bsup_ref[q], preferred_element_type=F32)
        for r in range(S5_SUPER_CHUNKS):
            bur_ref[q * S5_SUPER_CHUNKS + r] = bu[:, r * LANES:(r + 1) * LANES]
            bui_ref[q * S5_SUPER_CHUNKS + r] = bu[:, S5_SUPER_MODES + r * LANES:S5_SUPER_MODES + (r + 1) * LANES]

    if chain:
        @pl.when(pl.program_id(1) == 0)
        def _():
            hor_ref[...] = h0r_ref[...]
            hoi_ref[...] = h0i_ref[...]

    step_in_tile = lax.broadcasted_iota(jnp.int32, (ntile, SUBLANES, LANES), 1)
    tile_idx = lax.broadcasted_iota(jnp.int32, (ntile, LANES), 0)
    tile_last = pl.ds(SUBLANES - 1, ntile, stride=SUBLANES)
    tile_steps = SUBLANES.bit_length() - 1

    def chunk(c, carry):
        pr = pwr_ref[c]
        pi = pwi_ref[c]
        xr = bur_ref[c].reshape(ntile, SUBLANES, LANES)
        xi = bui_ref[c].reshape(ntile, SUBLANES, LANES)
        for k in range(tile_steps):
            sh = 1 << k
            xr, xi = _cmul_add(xr, xi, pr[k:k + 1], pi[k:k + 1], pltpu.roll(xr, sh, 1), pltpu.roll(xi, sh, 1),
                               step_in_tile >= sh)
        p8r, p8i = pr[tile_steps:tile_steps + 1], pi[tile_steps:tile_steps + 1]
        if chain:
            bur_ref[c] = xr.reshape(rb, LANES)
            bui_ref[c] = xi.reshape(rb, LANES)
            tr = bur_ref[c, tile_last, :]
            ti = bui_ref[c, tile_last, :]
            hr, hi = hor_ref[c], hoi_ref[c]
            first = tile_idx == 0
            cr, ci = _cmul_add(tr, ti, p8r, p8i, hr, hi, first)
            for j in range(ntile.bit_length() - 1):
                sh = 1 << j
                k = tile_steps + j
                cr, ci = _cmul_add(cr, ci, pr[k:k + 1], pi[k:k + 1], pltpu.roll(cr, sh, 0), pltpu.roll(ci, sh, 0),
                                   tile_idx >= sh)
            prev_r = jnp.where(first, hr, pltpu.roll(cr, 1, 0))
            prev_i = jnp.where(first, hi, pltpu.roll(ci, 1, 0))
            hor_ref[c] = cr[ntile - 1:ntile]
            hoi_ref[c] = ci[ntile - 1:ntile]
        else:
            prev_r, prev_i = h0r_ref[c], h0i_ref[c]
        lr = lnr_ref[c]
        li = lni_ref[c]
        for k in range(ntile):
            rows = slice(k * SUBLANES, (k + 1) * SUBLANES)
            yr, yi = _cmul_add(xr[k], xi[k], lr, li, prev_r[k:k + 1], prev_i[k:k + 1])
            bur_ref[c, rows, :] = yr
            bui_ref[c, rows, :] = yi
        if not chain:
            hor_ref[c] = bur_ref[c, tile_last, :]
            hoi_ref[c] = bui_ref[c, tile_last, :]
        col = pl.multiple_of((c // S5_SUPER_CHUNKS) * 2 * S5_SUPER_MODES + (c % S5_SUPER_CHUNKS) * LANES, LANES)
        hs_ref[:, pl.ds(col, LANES)] = bur_ref[c].astype(BF16)
        hs_ref[:, pl.ds(col + S5_SUPER_MODES, LANES)] = bui_ref[c].astype(BF16)
        return carry

    lax.fori_loop(0, S5_LANE_CHUNKS, chunk, 0)

    y = jnp.concatenate(
        [jnp.dot(hs_ref[:, q * 2 * S5_SUPER_MODES:(q + 1) * 2 * S5_SUPER_MODES], csup_ref[q],
                 preferred_element_type=F32) for q in range(S5_SUPER)], axis=1) + d_ref[...] * u
    y = jax.nn.gelu(y)
    gate = jax.nn.sigmoid(_bdot(y, gw_ref[...]) + gb_ref[...])
    o_ref[...] = (y * gate).astype(o_ref.dtype)


def s5_mixer(proj, mix, row0, n_seq, seq_len, h0r, h0i, tabs, layer):
    chain = seq_len > SUBLANES
    rb = S5_BLOCK_ROWS
    if chain:
        grid = (n_seq, seq_len // rb)
        rows = lambda s, b: (row0 // rb + s * (seq_len // rb) + b, 0)
        h0r = h0r.reshape(S5_LANE_CHUNKS, n_seq, 1, LANES)
        h0i = h0i.reshape(S5_LANE_CHUNKS, n_seq, 1, LANES)
        st_spec = pl.BlockSpec((S5_LANE_CHUNKS, None, 1, LANES), lambda s, b: (0, s, 0, 0))
        st_shape = jax.ShapeDtypeStruct((S5_LANE_CHUNKS, n_seq, 1, LANES), F32)
    else:
        assert seq_len == SUBLANES
        nseg = rb // seq_len
        grid = (n_seq // nseg, 1)
        rows = lambda s, b: (row0 // rb + s, 0)
        st_spec = pl.BlockSpec((S5_LANE_CHUNKS, nseg, LANES), lambda s, b: (0, s, 0))
        st_shape = jax.ShapeDtypeStruct((S5_LANE_CHUNKS, n_seq, LANES), F32)
    const3 = lambda s, b: (layer, 0, 0)
    const4 = lambda s, b: (layer, 0, 0, 0)
    pw_spec = pl.BlockSpec((None, S5_LANE_CHUNKS, S5_POWERS, LANES), const4)
    lin_spec = pl.BlockSpec((None, S5_LANE_CHUNKS, SUBLANES, LANES), const4)
    mix, hr, hi = _pallas(
        functools.partial(_s5_kernel, rb=rb, chain=chain),
        grid=grid,
        in_specs=[pl.BlockSpec((rb, S5_WIDTH), rows), st_spec, st_spec, pw_spec, pw_spec, lin_spec, lin_spec,
                  pl.BlockSpec((None, S5_SUPER, S5_SUPER_CH, 2 * S5_SUPER_MODES), const4),
                  pl.BlockSpec((None, S5_SUPER, 2 * S5_SUPER_MODES, S5_SUPER_CH), const4),
                  pl.BlockSpec((None, 1, S5_WIDTH), const3),
                  pl.BlockSpec((None, S5_WIDTH, S5_WIDTH), const3),
                  pl.BlockSpec((None, 1, S5_WIDTH), const3)],
        args=[proj, h0r, h0i, tabs["pw_re"], tabs["pw_im"], tabs["lin_re"], tabs["lin_im"], tabs["bsup"],
              tabs["csup"], tabs["d"], tabs["glu_w"], tabs["glu_b"]],
        out_specs=[pl.BlockSpec((rb, S5_WIDTH), rows), st_spec, st_spec],
        out_shapes=[jax.ShapeDtypeStruct((proj.shape[0], MIX_WIDTH), BF16), st_shape, st_shape],
        carried=[mix],
        scratch=[pltpu.VMEM((S5_LANE_CHUNKS, rb, LANES), F32),
                 pltpu.VMEM((S5_LANE_CHUNKS, rb, LANES), F32),
                 pltpu.VMEM((rb, 2 * S5_MODES), BF16)],
        name="s5_mixer",
    )
    return mix, hr.reshape(S5_LANE_CHUNKS, n_seq, LANES), hi.reshape(S5_LANE_CHUNKS, n_seq, LANES)


def s5_tables(lam_re, lam_im, b_re, b_im, c_re, c_im, d, log_step, glu_w, glu_b):
    depth = lam_re.shape[0]
    step = jnp.exp(log_step.astype(F32))[..., None]
    lr, li = lam_re.astype(F32), lam_im.astype(F32)
    mag = jnp.exp(lr * step)
    lbr, lbi = mag * jnp.cos(li * step), mag * jnp.sin(li * step)
    den = lr * lr + li * li
    fr = ((lbr - 1.0) * lr + lbi * li) / den
    fi = (lbi * lr - (lbr - 1.0) * li) / den
    bbr = fr[..., None] * b_re - fi[..., None] * b_im
    bbi = fr[..., None] * b_im + fi[..., None] * b_re
    per = S5_GROUPS // S5_SUPER
    eye = jnp.eye(per, dtype=F32)

    def blk_in(t):
        t = t.reshape(depth, S5_SUPER, per, S5_STATE, S5_GROUP_CH)
        return jnp.einsum("dqgpc,gh->dqgchp", t, eye).reshape(depth, S5_SUPER, S5_SUPER_CH, S5_SUPER_MODES)

    def blk_out(t):
        t = t.reshape(depth, S5_SUPER, per, S5_GROUP_CH, S5_STATE)
        return jnp.einsum("dqgcp,gh->dqgphc", t, eye).reshape(depth, S5_SUPER, S5_SUPER_MODES, S5_SUPER_CH)

    bsup = jnp.concatenate([blk_in(bbr), blk_in(bbi)], axis=-1).astype(BF16)
    csup = jnp.concatenate([blk_out(c_re.astype(F32)), blk_out(-c_im.astype(F32))], axis=2).astype(BF16)
    pr, pi = lbr.reshape(depth, S5_MODES), lbi.reshape(depth, S5_MODES)
    prs, pis = [], []
    for _ in range(S5_POWERS):
        prs.append(pr)
        pis.append(pi)
        pr, pi = pr * pr - pi * pi, 2.0 * pr * pi
    lr, li = prs[0], pis[0]
    lrs, lis = [], []
    for _ in range(SUBLANES):
        lrs.append(lr)
        lis.append(li)
        lr, li = lr * prs[0] - li * pis[0], lr * pis[0] + li * prs[0]

    def chunked(ts):
        t = jnp.stack(ts, axis=1).reshape(depth, len(ts), S5_LANE_CHUNKS, LANES)
        return t.transpose(0, 2, 1, 3)

    return dict(pw_re=chunked(prs), pw_im=chunked(pis), lin_re=chunked(lrs), lin_im=chunked(lis),
                bsup=bsup, csup=csup, d=d.astype(F32).reshape(depth, 1, S5_WIDTH), glu_w=glu_w.astype(BF16),
                glu_b=glu_b.astype(F32).reshape(depth, 1, S5_WIDTH))


def _to_chunk_major(h):
    n = h.shape[0]
    return h.reshape(n, S5_LANE_CHUNKS, LANES).transpose(1, 0, 2)


def _from_chunk_major(h):
    n = h.shape[1]
    return h.transpose(1, 0, 2).reshape(n, S5_GROUPS, S5_STATE)


_RET_LOG_GAMMA = tuple(math.log(1.0 - 2.0 ** (-5.0 - h)) for h in range(RET_HEADS))
RET_WIDTH = RET_HEADS * RET_HEAD_DIM


def _ret_kernel(q_ref, k_ref, v_ref, g_ref, cos_ref, sin_ref, nw_ref, s0_ref, o_ref, s_ref, y_ref, *, t, nb):
    @pl.when(pl.program_id(1) == 0)
    def _():
        s_ref[...] = s0_ref[...]

    cos = cos_ref[...]
    sin = sin_ref[...]
    ii = lax.broadcasted_iota(jnp.int32, (t, t), 0)
    jj = lax.broadcasted_iota(jnp.int32, (t, t), 1)
    ti = lax.broadcasted_iota(jnp.int32, (t, 1), 0).astype(F32)
    half = RET_HEAD_DIM // 2
    for h in range(RET_HEADS):
        lg = _RET_LOG_GAMMA[h]
        cols = slice(h * RET_HEAD_DIM, (h + 1) * RET_HEAD_DIM)
        decay = jnp.exp(jnp.where(ii >= jj, (ii - jj).astype(F32) * lg, -jnp.inf))
        grow = jnp.exp((ti + 1.0) * lg)
        tail = jnp.exp((t - 1.0 - ti) * lg)
        for n in range(nb):
            rows = slice(n * t, (n + 1) * t)
            q = q_ref[rows, cols]
            k = k_ref[rows, cols]
            v = v_ref[rows, cols].astype(BF16)
            q = q * cos + pltpu.roll(q, half, 1) * sin
            k = (k * cos + pltpu.roll(k, half, 1) * sin) * (RET_HEAD_DIM ** -0.5)
            s_prev = s_ref[n, h]
            scores = lax.dot_general(q.astype(BF16), k.astype(BF16), _NT, preferred_element_type=F32) * decay
            y = _bdot(scores, v) + _bdot(q * grow, s_prev)
            s_ref[n, h] = math.exp(t * lg) * s_prev + lax.dot_general(
                (k * tail).astype(BF16), v, _TN, preferred_element_type=F32)
            y = y * lax.rsqrt(jnp.mean(y * y, axis=-1, keepdims=True) + EPS)
            y = y * nw_ref[:, cols]
            y_ref[rows, cols] = jax.nn.silu(g_ref[rows, cols]) * y
    o_ref[...] = y_ref[...].astype(o_ref.dtype)


def retention_mixer(proj, mix, row0, n_seq, seq_len, s0, s0_layer, s_out, depth, cos2, sin2, norm_w, layer):
    t = math.gcd(seq_len, CHUNK)
    n_chunks = seq_len // t
    nb = 1 if n_chunks > 1 else 8
    rb = nb * t

    def col(cb):
        return pl.BlockSpec((rb, RET_WIDTH), lambda s, c: (row0 // rb + s * n_chunks + c, cb))

    st_block = (None, nb, RET_HEADS, RET_HEAD_DIM, RET_HEAD_DIM)
    tab_spec = pl.BlockSpec((t, RET_HEAD_DIM), lambda s, c: (c, 0))
    return _pallas(
        functools.partial(_ret_kernel, t=t, nb=nb),
        grid=(n_seq // nb, n_chunks),
        in_specs=[col(1), col(2), col(3), col(4), tab_spec, tab_spec,
                  pl.BlockSpec((None, 1, RET_WIDTH), lambda s, c: (layer, 0, 0)),
                  pl.BlockSpec(st_block, lambda s, c: (s0_layer, s, 0, 0, 0))],
        args=[proj, proj, proj, proj, cos2, sin2, norm_w, s0],
        out_specs=[col(1), pl.BlockSpec(st_block, lambda s, c: (layer, s, 0, 0, 0))],
        out_shapes=[jax.ShapeDtypeStruct((proj.shape[0], MIX_WIDTH), BF16),
                    jax.ShapeDtypeStruct((depth,) + s0.shape[1:], F32)],
        carried=[mix, s_out],
        scratch=[pltpu.VMEM((rb, RET_WIDTH), F32)],
        name="retention_mixer",
    )


def rope_tables(pos):
    half = RET_HEAD_DIM // 2
    inv = ROPE_BASE ** (-jnp.arange(half, dtype=F32) / half)
    ang = pos.astype(F32)[:, None] * inv[None, :]
    cos, sin = jnp.cos(ang), jnp.sin(ang)
    return jnp.concatenate([cos, cos], axis=-1), jnp.concatenate([-sin, sin], axis=-1)


SSD_WIDTH = SSD_HEADS * SSD_HEAD_DIM
SSD_BC = SSD_GROUPS * SSD_STATE
SSD_CONV_DIM = SSD_WIDTH + 2 * SSD_BC
_SSD_PREV = SUBLANES
_COL_BLOCK = 512
_SSD_Z_COL = 2560
_SSD_XBC_COL = _SSD_Z_COL + SSD_WIDTH


def _ssd_kernel(z0_ref, z1_ref, x0_ref, x1_ref, x2_ref, dt_ref, cs_ref, cw_ref, cb_ref, dtb_ref, aneg_ref, dsk_ref,
                nw_ref, s0_ref, o_ref, s_ref, co_ref, ext_ref, xbc_ref, y_ref, xdt_ref, xtl_ref, fs_ref, *,
                t, nb, conv_rows):
    keep = SSD_CONV - 1
    rep = SSD_HEADS // SSD_GROUPS
    grp = rep * SSD_HEAD_DIM

    def conv_row(n):
        return pl.ds(n, 1) if conv_rows == nb else pl.ds(pl.program_id(0) * nb + n, 1)

    @pl.when(pl.program_id(1) == 0)
    def _():
        s_ref[...] = s0_ref[...]
        for n in range(nb):
            for r in range(keep):
                ext_ref[n, _SSD_PREV - keep + r:_SSD_PREV - keep + r + 1, :] = cs_ref[r, conv_row(n), :]

    rb = nb * t
    ii = lax.broadcasted_iota(jnp.int32, (rb, rb), 0)
    jj = lax.broadcasted_iota(jnp.int32, (rb, rb), 1)
    same_seq = (ii // t) == (jj // t)
    causal = jnp.logical_and(same_seq, ii >= jj)
    tri = causal.astype(BF16)
    seq_ones = same_seq.astype(BF16)
    cw = cw_ref[...]
    for n in range(nb):
        rows = slice(n * t, (n + 1) * t)
        ext_ref[n, _SSD_PREV:_SSD_PREV + t, 0:_COL_BLOCK] = x0_ref[rows, :]
        ext_ref[n, _SSD_PREV:_SSD_PREV + t, _COL_BLOCK:2 * _COL_BLOCK] = x1_ref[rows, :]
        ext_ref[n, _SSD_PREV:_SSD_PREV + t, 2 * _COL_BLOCK:] = x2_ref[rows, :]
        conv = cb_ref[...] + cw[keep:keep + 1] * ext_ref[n, _SSD_PREV:_SSD_PREV + t, :]
        for back in range(1, SSD_CONV):
            conv = conv + cw[keep - back:keep - back + 1] * ext_ref[n, pl.ds(_SSD_PREV - back, t), :]
        tail_rows = ext_ref[n, _SSD_PREV + t - keep:_SSD_PREV + t, :]
        for r in range(keep):
            co_ref[r, conv_row(n), :] = tail_rows[r:r + 1]
        ext_ref[n, _SSD_PREV - keep:_SSD_PREV, :] = tail_rows
        xbc_ref[rows, :] = jax.nn.silu(conv)

    dt = jax.nn.softplus(dt_ref[...] + dtb_ref[...])
    la = dt * aneg_ref[...]
    pieces = _split3(la)
    cum = sum(jnp.dot(tri, p, preferred_element_type=F32) for p in pieces)
    tot = sum(jnp.dot(seq_ones, p, preferred_element_type=F32) for p in pieces)
    cum_t = cum.T
    e_last = jnp.exp(tot)

    spread = (lax.broadcasted_iota(jnp.int32, (LANES, SSD_WIDTH), 0)
              == lax.broadcasted_iota(jnp.int32, (LANES, SSD_WIDTH), 1) // SSD_HEAD_DIM).astype(BF16)

    def per_column(v):
        return sum(jnp.dot(p, spread, preferred_element_type=F32) for p in _split3(v))

    xh = xbc_ref[:, :SSD_WIDTH]
    xdt_ref[...] = xh * per_column(dt)
    xtl_ref[...] = xh * per_column(dt * jnp.exp(tot - cum))
    e_cum_cols = per_column(jnp.exp(cum))
    first_of_pair = lax.broadcasted_iota(jnp.int32, (1, 2 * SSD_HEAD_DIM), 1) < SSD_HEAD_DIM
    for g in range(SSD_GROUPS):
        gc = slice(g * grp, (g + 1) * grp)
        bg = xbc_ref[:, SSD_WIDTH + g * SSD_STATE:SSD_WIDTH + (g + 1) * SSD_STATE]
        cg = xbc_ref[:, SSD_WIDTH + SSD_BC + g * SSD_STATE:SSD_WIDTH + SSD_BC + (g + 1) * SSD_STATE]
        gram = lax.dot_general(cg.astype(BF16), bg.astype(BF16), _NT, preferred_element_type=F32)
        for n in range(nb):
            rows = slice(n * t, (n + 1) * t)
            s_grp = s_ref[n, g * rep:(g + 1) * rep].reshape(grp, SSD_STATE)
            fs_ref[rows, gc] = lax.dot_general(cg[rows].astype(BF16), s_grp.astype(BF16), _NT,
                                               preferred_element_type=F32)
            s_add = lax.dot_general(xtl_ref[rows, gc].astype(BF16), bg[rows].astype(BF16), _TN,
                                    preferred_element_type=F32)
            for hh in range(rep):
                h = g * rep + hh
                s_ref[n, h] = (e_last[n * t:n * t + 1, h:h + 1] * s_ref[n, h]
                               + s_add[hh * SSD_HEAD_DIM:(hh + 1) * SSD_HEAD_DIM, :])
        for pair in range(rep // 2):
            h = g * rep + 2 * pair
            pc = slice(h * SSD_HEAD_DIM, (h + 2) * SSD_HEAD_DIM)
            xp = xdt_ref[:, pc]
            intra = 0.0
            for k in range(2):
                decay = jnp.exp(jnp.where(causal, cum[:, h + k:h + k + 1] - cum_t[h + k:h + k + 1, :], -jnp.inf))
                own = first_of_pair if k == 0 else jnp.logical_not(first_of_pair)
                intra = intra + _bdot(gram * decay, jnp.where(own, xp, 0.0))
            y_ref[:, pc] = intra
    y = y_ref[...] + e_cum_cols * fs_ref[...] + xh * dsk_ref[...]
    y = jnp.concatenate([y[:, :_COL_BLOCK] * jax.nn.silu(z0_ref[...]),
                         y[:, _COL_BLOCK:] * jax.nn.silu(z1_ref[...])], axis=1)
    y = y * lax.rsqrt(jnp.mean(y * y, axis=-1, keepdims=True) + EPS)
    o_ref[...] = (y * nw_ref[...]).astype(o_ref.dtype)


def ssd_mixer(proj, dt_raw, mix, row0, n_seq, seq_len, s0, conv0, st_layer, s_out, conv_out, depth, tabs, layer):
    t = math.gcd(seq_len, CHUNK)
    n_chunks = seq_len // t
    nb = 1 if n_chunks > 1 else SUBLANES
    rb = nb * t
    conv_rows = nb if nb % SUBLANES == 0 else n_seq

    def col(cb, width=_COL_BLOCK):
        return pl.BlockSpec((rb, width), lambda s, c: (row0 // rb + s * n_chunks + c, cb))

    def conv_idx(lyr):
        return lambda s, c: (lyr, 0, s if conv_rows == nb else 0, 0)

    z_cb = _SSD_Z_COL // _COL_BLOCK
    x_cb = _SSD_XBC_COL // _COL_BLOCK
    const3 = lambda s, c: (layer, 0, 0)
    st_block = (None, nb, SSD_HEADS, SSD_HEAD_DIM, SSD_STATE)
    cs_block = (None, SSD_CONV - 1, conv_rows, SSD_CONV_DIM)
    return _pallas(
        functools.partial(_ssd_kernel, t=t, nb=nb, conv_rows=conv_rows),
        grid=(n_seq // nb, n_chunks),
        in_specs=[col(z_cb), col(z_cb + 1), col(x_cb), col(x_cb + 1), col(x_cb + 2), col(0, LANES),
                  pl.BlockSpec(cs_block, conv_idx(st_layer)),
                  pl.BlockSpec((None, SSD_CONV, SSD_CONV_DIM), const3),
                  pl.BlockSpec((None, 1, SSD_CONV_DIM), const3),
                  pl.BlockSpec((None, 1, LANES), const3),
                  pl.BlockSpec((None, 1, LANES), const3),
                  pl.BlockSpec((None, 1, SSD_WIDTH), const3),
                  pl.BlockSpec((None, 1, SSD_WIDTH), const3),
                  pl.BlockSpec(st_block, lambda s, c: (st_layer, s, 0, 0, 0))],
        args=[proj, proj, proj, proj, proj, dt_raw, conv0, tabs["conv_w"], tabs["conv_b"], tabs["dt_bias"],
              tabs["a_neg"], tabs["d"], tabs["norm"], s0],
        out_specs=[col(1, SSD_WIDTH),
                   pl.BlockSpec(st_block, lambda s, c: (layer, s, 0, 0, 0)),
                   pl.BlockSpec(cs_block, conv_idx(layer))],
        out_shapes=[jax.ShapeDtypeStruct((proj.shape[0], MIX_WIDTH), BF16),
                    jax.ShapeDtypeStruct((depth,) + s0.shape[1:], F32),
                    jax.ShapeDtypeStruct((depth,) + conv0.shape[1:], F32)],
        carried=[mix, s_out, conv_out],
        scratch=[pltpu.VMEM((nb, _SSD_PREV + t, SSD_CONV_DIM), F32),
                 pltpu.VMEM((rb, SSD_CONV_DIM), F32)] + [pltpu.VMEM((rb, SSD_WIDTH), F32)] * 4,
        name="ssd_mixer",
    )


def _softmax(s):
    s = s - jnp.max(s, axis=-1, keepdims=True)
    p = jnp.exp(s)
    return p / jnp.sum(p, axis=-1, keepdims=True)


def _xattn_kernel(q_ref, k_ref, v_ref, o_ref, *, tq, nb, head_dim, heads_axis):
    if not heads_axis:
        for h in range(XATTN_HEADS):
            cols = slice(h * head_dim, (h + 1) * head_dim)
            q = q_ref[:, cols]
            pieces = []
            for n in range(nb):
                s = lax.dot_general(q, k_ref[n, :, cols].astype(BF16), _NT,
                                    preferred_element_type=F32)[n * tq:(n + 1) * tq]
                pieces.append(_bdot(_softmax(s * (head_dim ** -0.5)), v_ref[n, :, cols]))
            att = pieces[0] if nb == 1 else jnp.concatenate(pieces, axis=0)
            o_ref[:, cols] = att.astype(o_ref.dtype)
        return

    flat = MEM_TOKENS * XATTN_HEADS
    rows = XATTN_HEADS * tq
    row_head = lax.broadcasted_iota(jnp.int32, (rows, flat), 0) // tq
    col_head = lax.broadcasted_iota(jnp.int32, (rows, flat), 1) % XATTN_HEADS
    same_head = row_head == col_head
    q_all = q_ref[...].astype(F32)
    outs = []
    for n in range(nb):
        q = jnp.concatenate([q_all[n * tq:(n + 1) * tq, h * head_dim:(h + 1) * head_dim]
                             for h in range(XATTN_HEADS)], axis=0)
        k = k_ref[n].reshape(flat, head_dim)
        v = v_ref[n].reshape(flat, head_dim)
        s = lax.dot_general(q.astype(BF16), k.astype(BF16), _NT, preferred_element_type=F32)
        s = jnp.where(same_head, s * (head_dim ** -0.5), -jnp.inf)
        outs.append(_bdot(_softmax(s), v))
    for h in range(XATTN_HEADS):
        att = jnp.concatenate([o[h * tq:(h + 1) * tq] for o in outs], axis=0)
        o_ref[:, h * head_dim:(h + 1) * head_dim] = att.astype(o_ref.dtype)


def cross_attention(q, att, row0, n_seq, seq_len, mem_k, mem_v, layer=None):
    d = q.shape[-1]
    head_dim = d // XATTN_HEADS
    if seq_len > SUBLANES:
        tq, nb = 512, 1
    else:
        tq, nb = seq_len, 4
    n_q = seq_len // tq
    rb = nb * tq
    if layer is None:
        kv_spec = pl.BlockSpec((nb, MEM_TOKENS, d), lambda s, i: (s, 0, 0))
    else:
        kv_spec = pl.BlockSpec((None, nb, MEM_TOKENS, XATTN_HEADS, head_dim), lambda s, i: (layer, s, 0, 0, 0))
    q_spec = pl.BlockSpec((rb, d), lambda s, i: (row0 // rb + s * n_q + i, 0))
    (att,) = _pallas(
        functools.partial(_xattn_kernel, tq=tq, nb=nb, head_dim=head_dim, heads_axis=layer is not None),
        grid=(n_seq // nb, n_q),
        in_specs=[q_spec, kv_spec, kv_spec],
        args=[q, mem_k, mem_v],
        out_specs=[q_spec],
        out_shapes=[jax.ShapeDtypeStruct(q.shape, BF16)],
        carried=[att],
        name="cross_attention",
    )
    return att


def kernel(x_prompt, x_sample, mem_prompt, state_s5_re, state_s5_im, state_ret, state_ssm, state_conv, cache_mem_k, cache_mem_v, ffn1_norm, ffn1_w1, ffn1_w3, ffn1_w2, mix_norm, w_in, w_out, s5_lambda_re, s5_lambda_im, s5_b_re, s5_b_im, s5_c_re, s5_c_im, s5_d, s5_log_step, s5_glu_w, s5_glu_b, ret_norm, ssd_conv_w, ssd_conv_b, ssd_dt_bias, ssd_a_log, ssd_d, ssd_norm, xattn_norm, xattn_wq, xattn_wk, xattn_wv, xattn_wo, ffn2_norm, ffn2_w1, ffn2_w3, ffn2_w2, final_norm):
    bp, lp, d = x_prompt.shape
    bs, ls, _ = x_sample.shape
    depth = w_in.shape[0]
    ffn = ffn1_w1.shape[-1]
    n_p, n_s = bp * lp, bs * ls

    def gain(g):
        return g.astype(F32).reshape(-1, 1, g.shape[-1])

    def lane_pad(v):
        return jnp.pad(v.astype(F32), ((0, 0), (0, LANES - v.shape[-1]))).reshape(depth, 1, LANES)

    s5_tabs = s5_tables(s5_lambda_re, s5_lambda_im, s5_b_re, s5_b_im, s5_c_re, s5_c_im, s5_d, s5_log_step,
                        s5_glu_w, s5_glu_b)
    main_cols = w_in.shape[-1] - SSD_HEADS
    w_dt = jnp.pad(w_in[:, :, main_cols:], ((0, 0), (0, 0), (0, LANES - SSD_HEADS)))
    w_in_t = w_in.swapaxes(1, 2)
    ssd_tabs = dict(conv_w=ssd_conv_w.astype(F32), conv_b=gain(ssd_conv_b), dt_bias=lane_pad(ssd_dt_bias),
                    a_neg=lane_pad(-jnp.exp(ssd_a_log.astype(F32))),
                    d=gain(jnp.repeat(ssd_d, SSD_HEAD_DIM, axis=-1)), norm=gain(ssd_norm))
    ret_w = gain(ret_norm)
    cos_p, sin_p = rope_tables(jnp.arange(lp, dtype=jnp.int32))
    cos_s, sin_s = rope_tables(PAST_LEN + jnp.arange(ls, dtype=jnp.int32))
    mem = mem_prompt.reshape(bp * MEM_TOKENS, d).astype(BF16)
    norms = {name: gain(g) for name, g in dict(ffn1=ffn1_norm, mix=mix_norm, xattn=xattn_norm, ffn2=ffn2_norm,
                                                final=final_norm).items()}

    ssm_t = state_ssm.swapaxes(-1, -2)
    conv_t = state_conv.swapaxes(1, 2)
    s5_re_cm = jnp.stack([_to_chunk_major(state_s5_re[l]) for l in range(depth)])
    s5_im_cm = jnp.stack([_to_chunk_major(state_s5_im[l]) for l in range(depth)])
    zeros_s5 = jnp.zeros((S5_LANE_CHUNKS, bp, LANES), F32)
    zeros_ret = jnp.zeros((1, bp) + state_ret.shape[2:], F32)
    zeros_ssm = jnp.zeros((1, bp) + ssm_t.shape[2:], F32)
    zeros_conv = jnp.zeros((1, SSD_CONV - 1, bp, SSD_CONV_DIM), F32)

    x = jnp.concatenate([x_prompt.reshape(n_p, d), x_sample.reshape(n_s, d)], axis=0)
    lists = {name: [] for name in ("p_s5re", "p_s5im", "p_mk", "p_mv", "s_s5re", "s_s5im")}
    p_ret = p_ssm = p_conv = s_ret = s_ssm = s_conv = None

    h = rmsnorm(x, norms["ffn1"], 0, BF16)
    for l in range(depth):
        a = matmul(h, [ffn1_w1, ffn1_w3], l, n_cols=ffn, out_dtype=BF16)
        x, h = matmul_residual_norm(a, ffn1_w2, l, x, norms["mix"], l, scale=0.5)

        proj = matmul(h, [w_in_t], l, n_cols=main_cols, out_dtype=F32, w_transposed=True)
        dt_raw = matmul(h, [w_dt], l, n_cols=LANES, out_dtype=F32)

        mix, hr_p, hi_p = s5_mixer(proj, None, 0, bp, lp, zeros_s5, zeros_s5, s5_tabs, l)
        mix, hr_s, hi_s = s5_mixer(proj, mix, n_p, bs, ls, s5_re_cm[l], s5_im_cm[l], s5_tabs, l)
        mix, p_ret = retention_mixer(proj, mix, 0, bp, lp, zeros_ret, 0, p_ret, depth, cos_p, sin_p, ret_w, l)
        mix, s_ret = retention_mixer(proj, mix, n_p, bs, ls, state_ret, l, s_ret, depth, cos_s, sin_s, ret_w, l)
        mix, p_ssm, p_conv = ssd_mixer(proj, dt_raw, mix, 0, bp, lp, zeros_ssm, zeros_conv, 0, p_ssm, p_conv,
                                       depth, ssd_tabs, l)
        mix, s_ssm, s_conv = ssd_mixer(proj, dt_raw, mix, n_p, bs, ls, ssm_t, conv_t, l, s_ssm, s_conv,
                                       depth, ssd_tabs, l)
        x, h = matmul_residual_norm(mix, w_out, l, x, norms["xattn"], l)
        q = matmul(h, [xattn_wq], l, n_cols=d, out_dtype=BF16)
        mk = matmul(mem, [xattn_wk], l, n_cols=d, out_dtype=F32)
        mv = matmul(mem, [xattn_wv], l, n_cols=d, out_dtype=F32)
        att = cross_attention(q, None, 0, bp, lp, mk.astype(BF16).reshape(bp, MEM_TOKENS, d),
                              mv.astype(BF16).reshape(bp, MEM_TOKENS, d))
        att = cross_attention(q, att, n_p, bs, ls, cache_mem_k, cache_mem_v, layer=l)
        x, h = matmul_residual_norm(att, xattn_wo, l, x, norms["ffn2"], l)
        a = matmul(h, [ffn2_w1, ffn2_w3], l, n_cols=ffn, out_dtype=BF16)
        if l + 1 < depth:
            x, h = matmul_residual_norm(a, ffn2_w2, l, x, norms["ffn1"], l + 1, scale=0.5)
        else:
            x = matmul(a, [ffn2_w2], l, n_cols=d, out_dtype=F32, res=x, scale=0.5)

        lists["p_s5re"].append(_from_chunk_major(hr_p))
        lists["p_s5im"].append(_from_chunk_major(hi_p))
        lists["p_mk"].append(mk.reshape(bp, MEM_TOKENS, XATTN_HEADS, d // XATTN_HEADS))
        lists["p_mv"].append(mv.reshape(bp, MEM_TOKENS, XATTN_HEADS, d // XATTN_HEADS))
        lists["s_s5re"].append(_from_chunk_major(hr_s))
        lists["s_s5im"].append(_from_chunk_major(hi_s))

    y_p = rmsnorm(x, norms["final"], 0, F32, 0, n_p)
    y_s = rmsnorm(x, norms["final"], 0, F32, n_p, n_s)
    st = {name: jnp.stack(v) for name, v in lists.items()}
    return (y_p.reshape(bp, lp, d), y_s.reshape(bs, ls, d),
            st["p_s5re"], st["p_s5im"], p_ret, p_ssm.swapaxes(-1, -2), p_conv.swapaxes(1, 2),
            st["p_mk"], st["p_mv"],
            st["s_s5re"], st["s_s5im"], s_ret, s_ssm.swapaxes(-1, -2), s_conv.swapaxes(1, 2))
```

```python
import functools
import math

import jax
import jax.numpy as jnp
from jax import lax
from jax.experimental import pallas as pl
from jax.experimental.pallas import tpu as pltpu

F32 = jnp.float32
BF16 = jnp.bfloat16

EPS = 1e-6
S5_GROUPS = 32
S5_GROUP_CH = 16
S5_STATE = 64
S5_MODES = S5_GROUPS * S5_STATE
RET_HEADS = 4
RET_HEAD_DIM = 128
ROPE_BASE = 10000.0
SSD_HEADS = 16
SSD_HEAD_DIM = 64
SSD_GROUPS = 2
SSD_STATE = 128
SSD_CONV = 4
MEM_TOKENS = 256
XATTN_HEADS = 4
CHUNK = 128
PAST_LEN = 16384

MIX_WIDTH = 2048

LANES = 128
SUBLANES = 8
MIB = 1024 * 1024
VMEM_LIMIT_BYTES = 60 * MIB
MATMUL_VMEM_BUDGET = VMEM_LIMIT_BYTES - 10 * MIB

_NT = (((1,), (1,)), ((), ()))
_TN = (((0,), (0,)), ((), ()))


def _params(n_axes):
    return pltpu.CompilerParams(dimension_semantics=("arbitrary",) * n_axes,
                                vmem_limit_bytes=VMEM_LIMIT_BYTES)


def _pallas(body, *, grid, in_specs, args, out_specs, out_shapes, carried=(), scratch=(), name):
    carried = list(carried) + [None] * (len(out_shapes) - len(carried))
    extra = [(i, c) for i, c in enumerate(carried) if c is not None]
    n_in = len(args)

    def with_carried(*refs):
        body(*refs[:n_in], *refs[n_in + len(extra):])

    return pl.pallas_call(
        with_carried,
        grid=grid,
        in_specs=list(in_specs) + [pl.BlockSpec(memory_space=pl.ANY)] * len(extra),
        out_specs=out_specs,
        out_shape=out_shapes,
        input_output_aliases={n_in + k: i for k, (i, _) in enumerate(extra)},
        scratch_shapes=list(scratch),
        compiler_params=_params(len(grid)),
        name=name,
    )(*args, *[c for _, c in extra])


def _bdot(a, b):
    return jnp.dot(a.astype(BF16), b.astype(BF16), preferred_element_type=F32)


def _split3(x):
    x1 = x.astype(BF16)
    r1 = x - x1.astype(F32)
    x2 = r1.astype(BF16)
    x3 = (r1 - x2.astype(F32)).astype(BF16)
    return x1, x2, x3


def _rmsnorm_kernel(x_ref, g_ref, o_ref):
    x = x_ref[...]
    ms = jnp.mean(x * x, axis=-1, keepdims=True)
    o_ref[...] = ((x * lax.rsqrt(ms + EPS)) * g_ref[...]).astype(o_ref.dtype)


def rmsnorm(x, gains, layer, out_dtype, row0=0, n_rows=None, tm=1024):
    d = x.shape[1]
    m = x.shape[0] if n_rows is None else n_rows
    return pl.pallas_call(
        _rmsnorm_kernel,
        grid=(m // tm,),
        in_specs=[pl.BlockSpec((tm, d), lambda i: (row0 // tm + i, 0)),
                  pl.BlockSpec((None, 1, d), lambda i: (layer, 0, 0))],
        out_specs=pl.BlockSpec((tm, d), lambda i: (i, 0)),
        out_shape=jax.ShapeDtypeStruct((m, d), out_dtype),
        compiler_params=_params(1),
        name="rmsnorm",
    )(x, gains)


def _mm_kernel(*refs, n_w, has_res, scale, w_transposed, sub_rows):
    a_ref = refs[0]
    w_refs = refs[1:1 + n_w]
    res_ref = refs[1 + n_w] if has_res else None
    o_ref = refs[1 + n_w + int(has_res)]
    wb_refs = refs[2 + n_w + int(has_res):]

    @pl.when(pl.program_id(1) == 0)
    def _():
        for w_ref, wb_ref in zip(w_refs, wb_refs):
            wb_ref[...] = w_ref[...].astype(BF16)

    def mm(a, wb_ref):
        if w_transposed:
            return lax.dot_general(a, wb_ref[...], _NT, preferred_element_type=F32)
        return jnp.dot(a, wb_ref[...], preferred_element_type=F32)

    def rows_pass(rows):
        a = a_ref[rows, :]
        acc = mm(a, wb_refs[0])
        if n_w == 2:
            acc = jax.nn.silu(acc) * mm(a, wb_refs[1])
        if has_res:
            acc = res_ref[rows, :] + (acc if scale == 1.0 else scale * acc)
        o_ref[rows, :] = acc.astype(o_ref.dtype)

    tm = a_ref.shape[0]
    if tm <= sub_rows:
        rows_pass(slice(None))
    else:
        def body(r, carry):
            rows_pass(pl.ds(pl.multiple_of(r * sub_rows, sub_rows), sub_rows))
            return carry
        lax.fori_loop(0, tm // sub_rows, body, 0)


_MATMUL_ROW_TILES = ((2304, 1152), (1024, 1024), (512, 512), (256, 256))


def _matmul_tiles(m, k, n_cols, n_w, has_res, out_bytes):
    for tn in (1024, 512, 256, 128):
        if n_cols % tn:
            continue
        for tm, sub_rows in _MATMUL_ROW_TILES:
            if m % tm:
                continue
            need = (2 * tm * k * 2
                    + n_w * (2 * k * tn * 4 + k * tn * 2)
                    + 2 * tm * tn * out_bytes + (2 * tm * tn * 4 if has_res else 0)
                    + n_w * sub_rows * tn * 4)
            if need <= MATMUL_VMEM_BUDGET:
                return tn, tm, sub_rows
    raise ValueError("no matmul tiling fits VMEM")


def matmul(a, ws, layer, *, n_cols, out_dtype, res=None, scale=1.0, w_transposed=False):
    m, k = a.shape
    n_w = len(ws)
    tn, tm, sub_rows = _matmul_tiles(m, k, n_cols, n_w, res is not None, jnp.dtype(out_dtype).itemsize)
    in_specs = [pl.BlockSpec((tm, k), lambda j, i: (i, 0))]
    if w_transposed:
        in_specs += [pl.BlockSpec((None, tn, k), lambda j, i: (layer, j, 0)) for _ in ws]
    else:
        in_specs += [pl.BlockSpec((None, k, tn), lambda j, i: (layer, 0, j)) for _ in ws]
    args = [a, *ws]
    if res is not None:
        in_specs.append(pl.BlockSpec((tm, tn), lambda j, i: (i, j)))
        args.append(res)
    return pl.pallas_call(
        functools.partial(_mm_kernel, n_w=n_w, has_res=res is not None, scale=scale, w_transposed=w_transposed,
                          sub_rows=sub_rows),
        grid=(n_cols // tn, m // tm),
        in_specs=in_specs,
        out_specs=pl.BlockSpec((tm, tn), lambda j, i: (i, j)),
        out_shape=jax.ShapeDtypeStruct((m, n_cols), out_dtype),
        scratch_shapes=[pltpu.VMEM((tn, k) if w_transposed else (k, tn), BF16) for _ in ws],
        compiler_params=_params(2),
        name="matmul",
    )(*args)


_ROWS_K_CHUNK = 512


def _mm_rows_kernel(a_ref, w_ref, res_ref, g_ref, x_ref, h_ref, wb_ref, *, n_pro, scale):
    s = pl.program_id(0)

    @pl.when(s < n_pro)
    def _():
        start = pl.multiple_of(s * _ROWS_K_CHUNK, _ROWS_K_CHUNK)
        wb_ref[pl.ds(start, _ROWS_K_CHUNK), :] = w_ref[...].astype(BF16)

    @pl.when(s >= n_pro)
    def _():
        acc = jnp.dot(a_ref[...], wb_ref[...], preferred_element_type=F32)
        x = res_ref[...] + (acc if scale == 1.0 else scale * acc)
        x_ref[...] = x
        ms = jnp.mean(x * x, axis=-1, keepdims=True)
        h_ref[...] = ((x * lax.rsqrt(ms + EPS)) * g_ref[...]).astype(h_ref.dtype)


def matmul_residual_norm(a, w, layer, res, gains, gain_layer, scale=1.0):
    m, k = a.shape
    n = res.shape[1]
    n_pro = k // _ROWS_K_CHUNK
    fixed = k * n * 2 + 2 * _ROWS_K_CHUNK * n * 4
    for tm in (512, 256, 128):
        per_tile = 2 * tm * k * 2 + 2 * tm * n * (4 + 4 + 2) + tm * n * 4
        if m % tm == 0 and fixed + per_tile <= MATMUL_VMEM_BUDGET:
            break
    else:
        raise ValueError("no row tile fits VMEM")
    row = lambda s: (jnp.maximum(s - n_pro, 0), 0)
    return pl.pallas_call(
        functools.partial(_mm_rows_kernel, n_pro=n_pro, scale=scale),
        grid=(n_pro + m // tm,),
        in_specs=[pl.BlockSpec((tm, k), row),
                  pl.BlockSpec((None, _ROWS_K_CHUNK, n), lambda s: (layer, jnp.minimum(s, n_pro - 1), 0)),
                  pl.BlockSpec((tm, n), row),
                  pl.BlockSpec((None, 1, n), lambda s: (gain_layer, 0, 0))],
        out_specs=[pl.BlockSpec((tm, n), row), pl.BlockSpec((tm, n), row)],
        out_shape=[jax.ShapeDtypeStruct((m, n), F32), jax.ShapeDtypeStruct((m, n), BF16)],
        scratch_shapes=[pltpu.VMEM((k, n), BF16)],
        compiler_params=_params(1),
        name="matmul_residual_norm",
    )(a, w, res, gains)


S5_LANE_CHUNKS = S5_MODES // LANES
S5_POWERS = 8
S5_WIDTH = S5_GROUPS * S5_GROUP_CH
S5_SUPER = 4
S5_SUPER_CH = S5_WIDTH // S5_SUPER
S5_SUPER_MODES = S5_MODES // S5_SUPER
S5_SUPER_CHUNKS = S5_SUPER_MODES // LANES
S5_BLOCK_ROWS = 256
S5_TILE_TABLE = 4 * SUBLANES


def _cmul_add(xr, xi, ar, ai, br, bi, keep=None):
    pr = ar * br - ai * bi
    pi = ar * bi + ai * br
    if keep is not None:
        pr = jnp.where(keep, pr, 0.0)
        pi = jnp.where(keep, pi, 0.0)
    return xr + pr, xi + pi


def _s5_kernel(u_ref, h0r_ref, h0i_ref, pwr_ref, pwi_ref, lnr_ref, lni_ref, bsup_ref, csup_ref, d_ref, gw_ref,
               gb_ref, o_ref, hor_ref, hoi_ref, bur_ref, bui_ref, hs_ref, *, rb, chain):
    ntile = rb // SUBLANES
    u = u_ref[...]
    ub = u.astype(BF16)
    for q in range(S5_SUPER):
        bu = jnp.dot(ub[:, q * S5_SUPER_CH:(q + 1) * S5_SUPER_CH], bsup_ref[q], preferred_element_type=F32)
        for r in range(S5_SUPER_CHUNKS):
            bur_ref[q * S5_SUPER_CHUNKS + r] = bu[:, r * LANES:(r + 1) * LANES]
            bui_ref[q * S5_SUPER_CHUNKS + r] = bu[:, S5_SUPER_MODES + r * LANES:S5_SUPER_MODES + (r + 1) * LANES]

    if chain:
        @pl.when(pl.program_id(1) == 0)
        def _():
            hor_ref[...] = h0r_ref[...]
            hoi_ref[...] = h0i_ref[...]

    tile_idx = lax.broadcasted_iota(jnp.int32, (ntile, LANES), 0)
    tile_last = pl.ds(SUBLANES - 1, ntile, stride=SUBLANES)
    tile_steps = SUBLANES.bit_length() - 1

    def chunk(c, carry):
        pr = pwr_ref[c]
        pi = pwi_ref[c]
        xr = bur_ref[c].reshape(ntile, SUBLANES, LANES)
        xi = bui_ref[c].reshape(ntile, SUBLANES, LANES)
        for k in range(tile_steps):
            sh = 1 << k
            rows = slice((k + 1) * SUBLANES, (k + 2) * SUBLANES)
            xr, xi = _cmul_add(xr, xi, lnr_ref[c, rows, :], lni_ref[c, rows, :],
                               pltpu.roll(xr, sh, 1), pltpu.roll(xi, sh, 1))
        p8r, p8i = pr[tile_steps:tile_steps + 1], pi[tile_steps:tile_steps + 1]
        if chain:
            bur_ref[c] = xr.reshape(rb, LANES)
            bui_ref[c] = xi.reshape(rb, LANES)
            tr = bur_ref[c, tile_last, :]
            ti = bui_ref[c, tile_last, :]
            hr, hi = hor_ref[c], hoi_ref[c]
            first = tile_idx == 0
            cr, ci = _cmul_add(tr, ti, p8r, p8i, hr, hi, first)
            for j in range(ntile.bit_length() - 1):
                sh = 1 << j
                k = tile_steps + j
                cr, ci = _cmul_add(cr, ci, pr[k:k + 1], pi[k:k + 1], pltpu.roll(cr, sh, 0), pltpu.roll(ci, sh, 0),
                                   tile_idx >= sh)
            prev_r = jnp.where(first, hr, pltpu.roll(cr, 1, 0))
            prev_i = jnp.where(first, hi, pltpu.roll(ci, 1, 0))
            hor_ref[c] = cr[ntile - 1:ntile]
            hoi_ref[c] = ci[ntile - 1:ntile]
        else:
            prev_r, prev_i = h0r_ref[c], h0i_ref[c]
        lr = lnr_ref[c, 0:SUBLANES, :]
        li = lni_ref[c, 0:SUBLANES, :]
        for k in range(ntile):
            rows = slice(k * SUBLANES, (k + 1) * SUBLANES)
            yr, yi = _cmul_add(xr[k], xi[k], lr, li, prev_r[k:k + 1], prev_i[k:k + 1])
            bur_ref[c, rows, :] = yr
            bui_ref[c, rows, :] = yi
        if not chain:
            hor_ref[c] = bur_ref[c, tile_last, :]
            hoi_ref[c] = bui_ref[c, tile_last, :]
        col = pl.multiple_of((c // S5_SUPER_CHUNKS) * 2 * S5_SUPER_MODES + (c % S5_SUPER_CHUNKS) * LANES, LANES)
        hs_ref[:, pl.ds(col, LANES)] = bur_ref[c].astype(BF16)
        hs_ref[:, pl.ds(col + S5_SUPER_MODES, LANES)] = bui_ref[c].astype(BF16)
        return carry

    lax.fori_loop(0, S5_LANE_CHUNKS, chunk, 0)

    y = jnp.concatenate(
        [jnp.dot(hs_ref[:, q * 2 * S5_SUPER_MODES:(q + 1) * 2 * S5_SUPER_MODES], csup_ref[q],
                 preferred_element_type=F32) for q in range(S5_SUPER)], axis=1) + d_ref[...] * u
    y = jax.nn.gelu(y)
    gate = jax.nn.sigmoid(_bdot(y, gw_ref[...]) + gb_ref[...])
    o_ref[...] = (y * gate).astype(o_ref.dtype)


def s5_mixer(proj, mix, row0, n_seq, seq_len, h0r, h0i, tabs, layer):
    chain = seq_len > SUBLANES
    rb = S5_BLOCK_ROWS
    if chain:
        grid = (n_seq, seq_len // rb)
        rows = lambda s, b: (row0 // rb + s * (seq_len // rb) + b, 0)
        h0r = h0r.reshape(S5_LANE_CHUNKS, n_seq, 1, LANES)
        h0i = h0i.reshape(S5_LANE_CHUNKS, n_seq, 1, LANES)
        st_spec = pl.BlockSpec((S5_LANE_CHUNKS, None, 1, LANES), lambda s, b: (0, s, 0, 0))
        st_shape = jax.ShapeDtypeStruct((S5_LANE_CHUNKS, n_seq, 1, LANES), F32)
    else:
        assert seq_len == SUBLANES
        nseg = rb // seq_len
        grid = (n_seq // nseg, 1)
        rows = lambda s, b: (row0 // rb + s, 0)
        st_spec = pl.BlockSpec((S5_LANE_CHUNKS, nseg, LANES), lambda s, b: (0, s, 0))
        st_shape = jax.ShapeDtypeStruct((S5_LANE_CHUNKS, n_seq, LANES), F32)
    const3 = lambda s, b: (layer, 0, 0)
    const4 = lambda s, b: (layer, 0, 0, 0)
    pw_spec = pl.BlockSpec((None, S5_LANE_CHUNKS, S5_POWERS, LANES), const4)
    lin_spec = pl.BlockSpec((None, S5_LANE_CHUNKS, S5_TILE_TABLE, LANES), const4)
    mix, hr, hi = _pallas(
        functools.partial(_s5_kernel, rb=rb, chain=chain),
        grid=grid,
        in_specs=[pl.BlockSpec((rb, S5_WIDTH), rows), st_spec, st_spec, pw_spec, pw_spec, lin_spec, lin_spec,
                  pl.BlockSpec((None, S5_SUPER, S5_SUPER_CH, 2 * S5_SUPER_MODES), const4),
                  pl.BlockSpec((None, S5_SUPER, 2 * S5_SUPER_MODES, S5_SUPER_CH), const4),
                  pl.BlockSpec((None, 1, S5_WIDTH), const3),
                  pl.BlockSpec((None, S5_WIDTH, S5_WIDTH), const3),
                  pl.BlockSpec((None, 1, S5_WIDTH), const3)],
        args=[proj, h0r, h0i, tabs["pw_re"], tabs["pw_im"], tabs["lin_re"], tabs["lin_im"], tabs["bsup"],
              tabs["csup"], tabs["d"], tabs["glu_w"], tabs["glu_b"]],
        out_specs=[pl.BlockSpec((rb, S5_WIDTH), rows), st_spec, st_spec],
        out_shapes=[jax.ShapeDtypeStruct((proj.shape[0], MIX_WIDTH), BF16), st_shape, st_shape],
        carried=[mix],
        scratch=[pltpu.VMEM((S5_LANE_CHUNKS, rb, LANES), F32),
                 pltpu.VMEM((S5_LANE_CHUNKS, rb, LANES), F32),
                 pltpu.VMEM((rb, 2 * S5_MODES), BF16)],
        name="s5_mixer",
    )
    return mix, hr.reshape(S5_LANE_CHUNKS, n_seq, LANES), hi.reshape(S5_LANE_CHUNKS, n_seq, LANES)


def s5_tables(lam_re, lam_im, b_re, b_im, c_re, c_im, d, log_step, glu_w, glu_b):
    depth = lam_re.shape[0]
    step = jnp.exp(log_step.astype(F32))[..., None]
    lr, li = lam_re.astype(F32), lam_im.astype(F32)
    mag = jnp.exp(lr * step)
    lbr, lbi = mag * jnp.cos(li * step), mag * jnp.sin(li * step)
    den = lr * lr + li * li
    fr = ((lbr - 1.0) * lr + lbi * li) / den
    fi = (lbi * lr - (lbr - 1.0) * li) / den
    bbr = fr[..., None] * b_re - fi[..., None] * b_im
    bbi = fr[..., None] * b_im + fi[..., None] * b_re
    per = S5_GROUPS // S5_SUPER
    eye = jnp.eye(per, dtype=F32)

    def blk_in(t):
        t = t.reshape(depth, S5_SUPER, per, S5_STATE, S5_GROUP_CH)
        return jnp.einsum("dqgpc,gh->dqgchp", t, eye).reshape(depth, S5_SUPER, S5_SUPER_CH, S5_SUPER_MODES)

    def blk_out(t):
        t = t.reshape(depth, S5_SUPER, per, S5_GROUP_CH, S5_STATE)
        return jnp.einsum("dqgcp,gh->dqgphc", t, eye).reshape(depth, S5_SUPER, S5_SUPER_MODES, S5_SUPER_CH)

    bsup = jnp.concatenate([blk_in(bbr), blk_in(bbi)], axis=-1).astype(BF16)
    csup = jnp.concatenate([blk_out(c_re.astype(F32)), blk_out(-c_im.astype(F32))], axis=2).astype(BF16)
    pr, pi = lbr.reshape(depth, S5_MODES), lbi.reshape(depth, S5_MODES)
    prs, pis = [], []
    for _ in range(S5_POWERS):
        prs.append(pr)
        pis.append(pi)
        pr, pi = pr * pr - pi * pi, 2.0 * pr * pi
    lr, li = prs[0], pis[0]
    lrs, lis = [], []
    for _ in range(SUBLANES):
        lrs.append(lr)
        lis.append(li)
        lr, li = lr * prs[0] - li * pis[0], lr * pis[0] + li * prs[0]
    for k in range(SUBLANES.bit_length() - 1):
        for t in range(SUBLANES):
            lrs.append(prs[k] if t >= (1 << k) else jnp.zeros_like(prs[k]))
            lis.append(pis[k] if t >= (1 << k) else jnp.zeros_like(pis[k]))

    def chunked(ts):
        t = jnp.stack(ts, axis=1).reshape(depth, len(ts), S5_LANE_CHUNKS, LANES)
        return t.transpose(0, 2, 1, 3)

    return dict(pw_re=chunked(prs), pw_im=chunked(pis), lin_re=chunked(lrs), lin_im=chunked(lis),
                bsup=bsup, csup=csup, d=d.astype(F32).reshape(depth, 1, S5_WIDTH), glu_w=glu_w.astype(BF16),
                glu_b=glu_b.astype(F32).reshape(depth, 1, S5_WIDTH))


def _to_chunk_major(h):
    n = h.shape[0]
    return h.reshape(n, S5_LANE_CHUNKS, LANES).transpose(1, 0, 2)


def _from_chunk_major(h):
    n = h.shape[1]
    return h.transpose(1, 0, 2).reshape(n, S5_GROUPS, S5_STATE)


_RET_LOG_GAMMA = tuple(math.log(1.0 - 2.0 ** (-5.0 - h)) for h in range(RET_HEADS))
RET_WIDTH = RET_HEADS * RET_HEAD_DIM


def _ret_kernel(q_ref, k_ref, v_ref, g_ref, cos_ref, sin_ref, nw_ref, s0_ref, o_ref, s_ref, y_ref, *, t, nb):
    @pl.when(pl.program_id(1) == 0)
    def _():
        s_ref[...] = s0_ref[...]

    cos = cos_ref[...]
    sin = sin_ref[...]
    ii = lax.broadcasted_iota(jnp.int32, (t, t), 0)
    jj = lax.broadcasted_iota(jnp.int32, (t, t), 1)
    ti = lax.broadcasted_iota(jnp.int32, (t, 1), 0).astype(F32)
    half = RET_HEAD_DIM // 2
    for h in range(RET_HEADS):
        lg = _RET_LOG_GAMMA[h]
        cols = slice(h * RET_HEAD_DIM, (h + 1) * RET_HEAD_DIM)
        decay = jnp.exp(jnp.where(ii >= jj, (ii - jj).astype(F32) * lg, -jnp.inf))
        grow = jnp.exp((ti + 1.0) * lg)
        tail = jnp.exp((t - 1.0 - ti) * lg)
        for n in range(nb):
            rows = slice(n * t, (n + 1) * t)
            q = q_ref[rows, cols]
            k = k_ref[rows, cols]
            v = v_ref[rows, cols].astype(BF16)
            q = q * cos + pltpu.roll(q, half, 1) * sin
            k = (k * cos + pltpu.roll(k, half, 1) * sin) * (RET_HEAD_DIM ** -0.5)
            s_prev = s_ref[n, h]
            scores = lax.dot_general(q.astype(BF16), k.astype(BF16), _NT, preferred_element_type=F32) * decay
            y = _bdot(scores, v) + _bdot(q * grow, s_prev)
            s_ref[n, h] = math.exp(t * lg) * s_prev + lax.dot_general(
                (k * tail).astype(BF16), v, _TN, preferred_element_type=F32)
            y = y * lax.rsqrt(jnp.mean(y * y, axis=-1, keepdims=True) + EPS)
            y = y * nw_ref[:, cols]
            y_ref[rows, cols] = jax.nn.silu(g_ref[rows, cols]) * y
    o_ref[...] = y_ref[...].astype(o_ref.dtype)


def retention_mixer(proj, mix, row0, n_seq, seq_len, s0, s0_layer, s_out, depth, cos2, sin2, norm_w, layer):
    t = math.gcd(seq_len, CHUNK)
    n_chunks = seq_len // t
    nb = 1 if n_chunks > 1 else 8
    rb = nb * t

    def col(cb):
        return pl.BlockSpec((rb, RET_WIDTH), lambda s, c: (row0 // rb + s * n_chunks + c, cb))

    st_block = (None, nb, RET_HEADS, RET_HEAD_DIM, RET_HEAD_DIM)
    tab_spec = pl.BlockSpec((t, RET_HEAD_DIM), lambda s, c: (c, 0))
    return _pallas(
        functools.partial(_ret_kernel, t=t, nb=nb),
        grid=(n_seq // nb, n_chunks),
        in_specs=[col(1), col(2), col(3), col(4), tab_spec, tab_spec,
                  pl.BlockSpec((None, 1, RET_WIDTH), lambda s, c: (layer, 0, 0)),
                  pl.BlockSpec(st_block, lambda s, c: (s0_layer, s, 0, 0, 0))],
        args=[proj, proj, proj, proj, cos2, sin2, norm_w, s0],
        out_specs=[col(1), pl.BlockSpec(st_block, lambda s, c: (layer, s, 0, 0, 0))],
        out_shapes=[jax.ShapeDtypeStruct((proj.shape[0], MIX_WIDTH), BF16),
                    jax.ShapeDtypeStruct((depth,) + s0.shape[1:], F32)],
        carried=[mix, s_out],
        scratch=[pltpu.VMEM((rb, RET_WIDTH), F32)],
        name="retention_mixer",
    )


def rope_tables(pos):
    half = RET_HEAD_DIM // 2
    inv = ROPE_BASE ** (-jnp.arange(half, dtype=F32) / half)
    ang = pos.astype(F32)[:, None] * inv[None, :]
    cos, sin = jnp.cos(ang), jnp.sin(ang)
    return jnp.concatenate([cos, cos], axis=-1), jnp.concatenate([-sin, sin], axis=-1)


SSD_WIDTH = SSD_HEADS * SSD_HEAD_DIM
SSD_BC = SSD_GROUPS * SSD_STATE
SSD_CONV_DIM = SSD_WIDTH + 2 * SSD_BC
_SSD_PREV = SUBLANES
_COL_BLOCK = 512
_SSD_Z_COL = 2560
_SSD_XBC_COL = _SSD_Z_COL + SSD_WIDTH


def _ssd_kernel(z0_ref, z1_ref, x0_ref, x1_ref, x2_ref, dt_ref, cs_ref, cw_ref, cb_ref, dtb_ref, aneg_ref, dsk_ref,
                nw_ref, s0_ref, o_ref, s_ref, co_ref, ext_ref, xbc_ref, y_ref, xdt_ref, xtl_ref, fs_ref, *,
                t, nb, conv_rows):
    keep = SSD_CONV - 1
    rep = SSD_HEADS // SSD_GROUPS
    grp = rep * SSD_HEAD_DIM

    def conv_row(n):
        return pl.ds(n, 1) if conv_rows == nb else pl.ds(pl.program_id(0) * nb + n, 1)

    @pl.when(pl.program_id(1) == 0)
    def _():
        s_ref[...] = s0_ref[...]
        for n in range(nb):
            for r in range(keep):
                ext_ref[n, _SSD_PREV - keep + r:_SSD_PREV - keep + r + 1, :] = cs_ref[r, conv_row(n), :]

    rb = nb * t
    ii = lax.broadcasted_iota(jnp.int32, (rb, rb), 0)
    jj = lax.broadcasted_iota(jnp.int32, (rb, rb), 1)
    same_seq = (ii // t) == (jj // t)
    causal = jnp.logical_and(same_seq, ii >= jj)
    tri = causal.astype(BF16)
    seq_ones = same_seq.astype(BF16)
    cw = cw_ref[...]
    for n in range(nb):
        rows = slice(n * t, (n + 1) * t)
        ext_ref[n, _SSD_PREV:_SSD_PREV + t, 0:_COL_BLOCK] = x0_ref[rows, :]
        ext_ref[n, _SSD_PREV:_SSD_PREV + t, _COL_BLOCK:2 * _COL_BLOCK] = x1_ref[rows, :]
        ext_ref[n, _SSD_PREV:_SSD_PREV + t, 2 * _COL_BLOCK:] = x2_ref[rows, :]
        conv = cb_ref[...] + cw[keep:keep + 1] * ext_ref[n, _SSD_PREV:_SSD_PREV + t, :]
        for back in range(1, SSD_CONV):
            conv = conv + cw[keep - back:keep - back + 1] * ext_ref[n, pl.ds(_SSD_PREV - back, t), :]
        tail_rows = ext_ref[n, _SSD_PREV + t - keep:_SSD_PREV + t, :]
        for r in range(keep):
            co_ref[r, conv_row(n), :] = tail_rows[r:r + 1]
        ext_ref[n, _SSD_PREV - keep:_SSD_PREV, :] = tail_rows
        xbc_ref[rows, :] = jax.nn.silu(conv)

    dt = jax.nn.softplus(dt_ref[...] + dtb_ref[...])
    la = dt * aneg_ref[...]
    sums = jnp.dot(jnp.concatenate([jnp.concatenate([tri] * 3, axis=1), jnp.concatenate([seq_ones] * 3, axis=1)],
                                   axis=0),
                   jnp.concatenate(_split3(la), axis=0), preferred_element_type=F32)
    cum, tot = sums[:rb], sums[rb:]
    cum_t = cum.T
    e_last = jnp.exp(tot)

    spread = (lax.broadcasted_iota(jnp.int32, (3 * LANES, SSD_WIDTH), 0) % LANES
              == lax.broadcasted_iota(jnp.int32, (3 * LANES, SSD_WIDTH), 1) // SSD_HEAD_DIM).astype(BF16)
    factors = jnp.concatenate([dt, dt * jnp.exp(tot - cum), jnp.exp(cum)], axis=0)
    cols = jnp.dot(jnp.concatenate(_split3(factors), axis=1), spread, preferred_element_type=F32)
    xh = xbc_ref[:, :SSD_WIDTH]
    xdt_ref[...] = xh * cols[:rb]
    xtl_ref[...] = xh * cols[rb:2 * rb]
    e_cum_cols = cols[2 * rb:]
    first_of_pair = lax.broadcasted_iota(jnp.int32, (1, 2 * SSD_HEAD_DIM), 1) < SSD_HEAD_DIM
    for g in range(SSD_GROUPS):
        gc = slice(g * grp, (g + 1) * grp)
        bg = xbc_ref[:, SSD_WIDTH + g * SSD_STATE:SSD_WIDTH + (g + 1) * SSD_STATE]
        cg = xbc_ref[:, SSD_WIDTH + SSD_BC + g * SSD_STATE:SSD_WIDTH + SSD_BC + (g + 1) * SSD_STATE]
        gram = lax.dot_general(cg.astype(BF16), bg.astype(BF16), _NT, preferred_element_type=F32)
        for n in range(nb):
            rows = slice(n * t, (n + 1) * t)
            s_grp = s_ref[n, g * rep:(g + 1) * rep].reshape(grp, SSD_STATE)
            fs_ref[rows, gc] = lax.dot_general(cg[rows].astype(BF16), s_grp.astype(BF16), _NT,
                                               preferred_element_type=F32)
            s_add = lax.dot_general(xtl_ref[rows, gc].astype(BF16), bg[rows].astype(BF16), _TN,
                                    preferred_element_type=F32)
            for hh in range(rep):
                h = g * rep + hh
                s_ref[n, h] = (e_last[n * t:n * t + 1, h:h + 1] * s_ref[n, h]
                               + s_add[hh * SSD_HEAD_DIM:(hh + 1) * SSD_HEAD_DIM, :])
        for pair in range(rep // 2):
            h = g * rep + 2 * pair
            pc = slice(h * SSD_HEAD_DIM, (h + 2) * SSD_HEAD_DIM)
            xp = xdt_ref[:, pc]
            scores = [gram * jnp.exp(jnp.where(causal, cum[:, k:k + 1] - cum_t[k:k + 1, :], -jnp.inf))
                      for k in (h, h + 1)]
            y_ref[:, pc] = _bdot(jnp.concatenate(scores, axis=1),
                                 jnp.concatenate([jnp.where(first_of_pair, xp, 0.0),
                                                  jnp.where(first_of_pair, 0.0, xp)], axis=0))
    y = y_ref[...] + e_cum_cols * fs_ref[...] + xh * dsk_ref[...]
    y = jnp.concatenate([y[:, :_COL_BLOCK] * jax.nn.silu(z0_ref[...]),
                         y[:, _COL_BLOCK:] * jax.nn.silu(z1_ref[...])], axis=1)
    y = y * lax.rsqrt(jnp.mean(y * y, axis=-1, keepdims=True) + EPS)
    o_ref[...] = (y * nw_ref[...]).astype(o_ref.dtype)


def ssd_mixer(proj, dt_raw, mix, row0, n_seq, seq_len, s0, conv0, st_layer, s_out, conv_out, depth, tabs, layer):
    t = math.gcd(seq_len, CHUNK)
    n_chunks = seq_len // t
    nb = 1 if n_chunks > 1 else SUBLANES
    rb = nb * t
    conv_rows = nb if nb % SUBLANES == 0 else n_seq

    def col(cb, width=_COL_BLOCK):
        return pl.BlockSpec((rb, width), lambda s, c: (row0 // rb + s * n_chunks + c, cb))

    def conv_idx(lyr):
        return lambda s, c: (lyr, 0, s if conv_rows == nb else 0, 0)

    z_cb = _SSD_Z_COL // _COL_BLOCK
    x_cb = _SSD_XBC_COL // _COL_BLOCK
    const3 = lambda s, c: (layer, 0, 0)
    st_block = (None, nb, SSD_HEADS, SSD_HEAD_DIM, SSD_STATE)
    cs_block = (None, SSD_CONV - 1, conv_rows, SSD_CONV_DIM)
    return _pallas(
        functools.partial(_ssd_kernel, t=t, nb=nb, conv_rows=conv_rows),
        grid=(n_seq // nb, n_chunks),
        in_specs=[col(z_cb), col(z_cb + 1), col(x_cb), col(x_cb + 1), col(x_cb + 2), col(0, LANES),
                  pl.BlockSpec(cs_block, conv_idx(st_layer)),
                  pl.BlockSpec((None, SSD_CONV, SSD_CONV_DIM), const3),
                  pl.BlockSpec((None, 1, SSD_CONV_DIM), const3),
                  pl.BlockSpec((None, 1, LANES), const3),
                  pl.BlockSpec((None, 1, LANES), const3),
                  pl.BlockSpec((None, 1, SSD_WIDTH), const3),
                  pl.BlockSpec((None, 1, SSD_WIDTH), const3),
                  pl.BlockSpec(st_block, lambda s, c: (st_layer, s, 0, 0, 0))],
        args=[proj, proj, proj, proj, proj, dt_raw, conv0, tabs["conv_w"], tabs["conv_b"], tabs["dt_bias"],
              tabs["a_neg"], tabs["d"], tabs["norm"], s0],
        out_specs=[col(1, SSD_WIDTH),
                   pl.BlockSpec(st_block, lambda s, c: (layer, s, 0, 0, 0)),
                   pl.BlockSpec(cs_block, conv_idx(layer))],
        out_shapes=[jax.ShapeDtypeStruct((proj.shape[0], MIX_WIDTH), BF16),
                    jax.ShapeDtypeStruct((depth,) + s0.shape[1:], F32),
                    jax.ShapeDtypeStruct((depth,) + conv0.shape[1:], F32)],
        carried=[mix, s_out, conv_out],
        scratch=[pltpu.VMEM((nb, _SSD_PREV + t, SSD_CONV_DIM), F32),
                 pltpu.VMEM((rb, SSD_CONV_DIM), F32)] + [pltpu.VMEM((rb, SSD_WIDTH), F32)] * 4,
        name="ssd_mixer",
    )


def _softmax(s):
    s = s - jnp.max(s, axis=-1, keepdims=True)
    p = jnp.exp(s)
    return p / jnp.sum(p, axis=-1, keepdims=True)


def _xattn_kernel(q_ref, k_ref, v_ref, o_ref, *, tq, nb, head_dim, heads_axis):
    if not heads_axis:
        for h in range(XATTN_HEADS):
            cols = slice(h * head_dim, (h + 1) * head_dim)
            q = q_ref[:, cols]
            pieces = []
            for n in range(nb):
                s = lax.dot_general(q, k_ref[n, :, cols].astype(BF16), _NT,
                                    preferred_element_type=F32)[n * tq:(n + 1) * tq]
                pieces.append(_bdot(_softmax(s * (head_dim ** -0.5)), v_ref[n, :, cols]))
            att = pieces[0] if nb == 1 else jnp.concatenate(pieces, axis=0)
            o_ref[:, cols] = att.astype(o_ref.dtype)
        return

    flat = MEM_TOKENS * XATTN_HEADS
    rows = XATTN_HEADS * tq
    row_head = lax.broadcasted_iota(jnp.int32, (rows, flat), 0) // tq
    col_head = lax.broadcasted_iota(jnp.int32, (rows, flat), 1) % XATTN_HEADS
    same_head = row_head == col_head
    q_all = q_ref[...].astype(F32)
    outs = []
    for n in range(nb):
        q = jnp.concatenate([q_all[n * tq:(n + 1) * tq, h * head_dim:(h + 1) * head_dim]
                             for h in range(XATTN_HEADS)], axis=0)
        k = k_ref[n].reshape(flat, head_dim)
        v = v_ref[n].reshape(flat, head_dim)
        s = lax.dot_general(q.astype(BF16), k.astype(BF16), _NT, preferred_element_type=F32)
        s = jnp.where(same_head, s * (head_dim ** -0.5), -jnp.inf)
        outs.append(_bdot(_softmax(s), v))
    for h in range(XATTN_HEADS):
        att = jnp.concatenate([o[h * tq:(h + 1) * tq] for o in outs], axis=0)
        o_ref[:, h * head_dim:(h + 1) * head_dim] = att.astype(o_ref.dtype)


def cross_attention(q, att, row0, n_seq, seq_len, mem_k, mem_v, layer=None):
    d = q.shape[-1]
    head_dim = d // XATTN_HEADS
    if seq_len > SUBLANES:
        tq, nb = 512, 1
    else:
        tq, nb = seq_len, 4
    n_q = seq_len // tq
    rb = nb * tq
    if layer is None:
        kv_spec = pl.BlockSpec((nb, MEM_TOKENS, d), lambda s, i: (s, 0, 0))
    else:
        kv_spec = pl.BlockSpec((None, nb, MEM_TOKENS, XATTN_HEADS, head_dim), lambda s, i: (layer, s, 0, 0, 0))
    q_spec = pl.BlockSpec((rb, d), lambda s, i: (row0 // rb + s * n_q + i, 0))
    (att,) = _pallas(
        functools.partial(_xattn_kernel, tq=tq, nb=nb, head_dim=head_dim, heads_axis=layer is not None),
        grid=(n_seq // nb, n_q),
        in_specs=[q_spec, kv_spec, kv_spec],
        args=[q, mem_k, mem_v],
        out_specs=[q_spec],
        out_shapes=[jax.ShapeDtypeStruct(q.shape, BF16)],
        carried=[att],
        name="cross_attention",
    )
    return att


def kernel(x_prompt, x_sample, mem_prompt, state_s5_re, state_s5_im, state_ret, state_ssm, state_conv, cache_mem_k, cache_mem_v, ffn1_norm, ffn1_w1, ffn1_w3, ffn1_w2, mix_norm, w_in, w_out, s5_lambda_re, s5_lambda_im, s5_b_re, s5_b_im, s5_c_re, s5_c_im, s5_d, s5_log_step, s5_glu_w, s5_glu_b, ret_norm, ssd_conv_w, ssd_conv_b, ssd_dt_bias, ssd_a_log, ssd_d, ssd_norm, xattn_norm, xattn_wq, xattn_wk, xattn_wv, xattn_wo, ffn2_norm, ffn2_w1, ffn2_w3, ffn2_w2, final_norm):
    bp, lp, d = x_prompt.shape
    bs, ls, _ = x_sample.shape
    depth = w_in.shape[0]
    ffn = ffn1_w1.shape[-1]
    n_p, n_s = bp * lp, bs * ls

    def gain(g):
        return g.astype(F32).reshape(-1, 1, g.shape[-1])

    def lane_pad(v):
        return jnp.pad(v.astype(F32), ((0, 0), (0, LANES - v.shape[-1]))).reshape(depth, 1, LANES)

    s5_tabs = s5_tables(s5_lambda_re, s5_lambda_im, s5_b_re, s5_b_im, s5_c_re, s5_c_im, s5_d, s5_log_step,
                        s5_glu_w, s5_glu_b)
    main_cols = w_in.shape[-1] - SSD_HEADS
    w_dt = jnp.pad(w_in[:, :, main_cols:], ((0, 0), (0, 0), (0, LANES - SSD_HEADS)))
    w_in_t = w_in.swapaxes(1, 2)
    ssd_tabs = dict(conv_w=ssd_conv_w.astype(F32), conv_b=gain(ssd_conv_b), dt_bias=lane_pad(ssd_dt_bias),
                    a_neg=lane_pad(-jnp.exp(ssd_a_log.astype(F32))),
                    d=gain(jnp.repeat(ssd_d, SSD_HEAD_DIM, axis=-1)), norm=gain(ssd_norm))
    ret_w = gain(ret_norm)
    cos_p, sin_p = rope_tables(jnp.arange(lp, dtype=jnp.int32))
    cos_s, sin_s = rope_tables(PAST_LEN + jnp.arange(ls, dtype=jnp.int32))
    mem = mem_prompt.reshape(bp * MEM_TOKENS, d).astype(BF16)
    norms = {name: gain(g) for name, g in dict(ffn1=ffn1_norm, mix=mix_norm, xattn=xattn_norm, ffn2=ffn2_norm,
                                                final=final_norm).items()}

    ssm_t = state_ssm.swapaxes(-1, -2)
    conv_t = state_conv.swapaxes(1, 2)
    s5_re_cm = jnp.stack([_to_chunk_major(state_s5_re[l]) for l in range(depth)])
    s5_im_cm = jnp.stack([_to_chunk_major(state_s5_im[l]) for l in range(depth)])
    zeros_s5 = jnp.zeros((S5_LANE_CHUNKS, bp, LANES), F32)
    zeros_ret = jnp.zeros((1, bp) + state_ret.shape[2:], F32)
    zeros_ssm = jnp.zeros((1, bp) + ssm_t.shape[2:], F32)
    zeros_conv = jnp.zeros((1, SSD_CONV - 1, bp, SSD_CONV_DIM), F32)

    x = jnp.concatenate([x_prompt.reshape(n_p, d), x_sample.reshape(n_s, d)], axis=0)
    lists = {name: [] for name in ("p_s5re", "p_s5im", "p_mk", "p_mv", "s_s5re", "s_s5im")}
    p_ret = p_ssm = p_conv = s_ret = s_ssm = s_conv = None

    h = rmsnorm(x, norms["ffn1"], 0, BF16)
    for l in range(depth):
        a = matmul(h, [ffn1_w1, ffn1_w3], l, n_cols=ffn, out_dtype=BF16)
        x, h = matmul_residual_norm(a, ffn1_w2, l, x, norms["mix"], l, scale=0.5)

        proj = matmul(h, [w_in_t], l, n_cols=main_cols, out_dtype=F32, w_transposed=True)
        dt_raw = matmul(h, [w_dt], l, n_cols=LANES, out_dtype=F32)

        mix, hr_p, hi_p = s5_mixer(proj, None, 0, bp, lp, zeros_s5, zeros_s5, s5_tabs, l)
        mix, hr_s, hi_s = s5_mixer(proj, mix, n_p, bs, ls, s5_re_cm[l], s5_im_cm[l], s5_tabs, l)
        mix, p_ret = retention_mixer(proj, mix, 0, bp, lp, zeros_ret, 0, p_ret, depth, cos_p, sin_p, ret_w, l)
        mix, s_ret = retention_mixer(proj, mix, n_p, bs, ls, state_ret, l, s_ret, depth, cos_s, sin_s, ret_w, l)
        mix, p_ssm, p_conv = ssd_mixer(proj, dt_raw, mix, 0, bp, lp, zeros_ssm, zeros_conv, 0, p_ssm, p_conv,
                                       depth, ssd_tabs, l)
        mix, s_ssm, s_conv = ssd_mixer(proj, dt_raw, mix, n_p, bs, ls, ssm_t, conv_t, l, s_ssm, s_conv,
                                       depth, ssd_tabs, l)
        x, h = matmul_residual_norm(mix, w_out, l, x, norms["xattn"], l)
        q = matmul(h, [xattn_wq], l, n_cols=d, out_dtype=BF16)
        mk = matmul(mem, [xattn_wk], l, n_cols=d, out_dtype=F32)
        mv = matmul(mem, [xattn_wv], l, n_cols=d, out_dtype=F32)
        att = cross_attention(q, None, 0, bp, lp, mk.astype(BF16).reshape(bp, MEM_TOKENS, d),
                              mv.astype(BF16).reshape(bp, MEM_TOKENS, d))
        att = cross_attention(q, att, n_p, bs, ls, cache_mem_k, cache_mem_v, layer=l)
        x, h = matmul_residual_norm(att, xattn_wo, l, x, norms["ffn2"], l)
        a = matmul(h, [ffn2_w1, ffn2_w3], l, n_cols=ffn, out_dtype=BF16)
        if l + 1 < depth:
            x, h = matmul_residual_norm(a, ffn2_w2, l, x, norms["ffn1"], l + 1, scale=0.5)
        else:
            x = matmul(a, [ffn2_w2], l, n_cols=d, out_dtype=F32, res=x, scale=0.5)

        lists["p_s5re"].append(_from_chunk_major(hr_p))
        lists["p_s5im"].append(_from_chunk_major(hi_p))
        lists["p_mk"].append(mk.reshape(bp, MEM_TOKENS, XATTN_HEADS, d // XATTN_HEADS))
        lists["p_mv"].append(mv.reshape(bp, MEM_TOKENS, XATTN_HEADS, d // XATTN_HEADS))
        lists["s_s5re"].append(_from_chunk_major(hr_s))
        lists["s_s5im"].append(_from_chunk_major(hi_s))

    y_p = rmsnorm(x, norms["final"], 0, F32, 0, n_p)
    y_s = rmsnorm(x, norms["final"], 0, F32, n_p, n_s)
    st = {name: jnp.stack(v) for name, v in lists.items()}
    return (y_p.reshape(bp, lp, d), y_s.reshape(bs, ls, d),
            st["p_s5re"], st["p_s5im"], p_ret, p_ssm.swapaxes(-1, -2), p_conv.swapaxes(1, 2),
            st["p_mk"], st["p_mv"],
            st["s_s5re"], st["s_s5im"], s_ret, s_ssm.swapaxes(-1, -2), s_conv.swapaxes(1, 2))
```

```python
import functools
import math

import jax
import jax.numpy as jnp
from jax import lax
from jax.experimental import pallas as pl
from jax.experimental.pallas import tpu as pltpu

F32 = jnp.float32
BF16 = jnp.bfloat16

EPS = 1e-6
S5_GROUPS = 32
S5_GROUP_CH = 16
S5_STATE = 64
S5_MODES = S5_GROUPS * S5_STATE
RET_HEADS = 4
RET_HEAD_DIM = 128
ROPE_BASE = 10000.0
SSD_HEADS = 16
SSD_HEAD_DIM = 64
SSD_GROUPS = 2
SSD_STATE = 128
SSD_CONV = 4
MEM_TOKENS = 256
XATTN_HEADS = 4
CHUNK = 128
PAST_LEN = 16384

MIX_WIDTH = 2048

LANES = 128
SUBLANES = 8
MIB = 1024 * 1024
VMEM_LIMIT_BYTES = 60 * MIB
MATMUL_VMEM_BUDGET = VMEM_LIMIT_BYTES - 10 * MIB

_NT = (((1,), (1,)), ((), ()))
_TN = (((0,), (0,)), ((), ()))


def _params(n_axes):
    return pltpu.CompilerParams(dimension_semantics=("arbitrary",) * n_axes,
                                vmem_limit_bytes=VMEM_LIMIT_BYTES)


def _pallas(body, *, grid, in_specs, args, out_specs, out_shapes, carried=(), scratch=(), name):
    carried = list(carried) + [None] * (len(out_shapes) - len(carried))
    extra = [(i, c) for i, c in enumerate(carried) if c is not None]
    n_in = len(args)

    def with_carried(*refs):
        body(*refs[:n_in], *refs[n_in + len(extra):])

    return pl.pallas_call(
        with_carried,
        grid=grid,
        in_specs=list(in_specs) + [pl.BlockSpec(memory_space=pl.ANY)] * len(extra),
        out_specs=out_specs,
        out_shape=out_shapes,
        input_output_aliases={n_in + k: i for k, (i, _) in enumerate(extra)},
        scratch_shapes=list(scratch),
        compiler_params=_params(len(grid)),
        name=name,
    )(*args, *[c for _, c in extra])


def _bdot(a, b):
    return jnp.dot(a.astype(BF16), b.astype(BF16), preferred_element_type=F32)


def _split3(x):
    x1 = x.astype(BF16)
    r1 = x - x1.astype(F32)
    x2 = r1.astype(BF16)
    x3 = (r1 - x2.astype(F32)).astype(BF16)
    return x1, x2, x3


def _rmsnorm_kernel(x_ref, g_ref, o_ref):
    x = x_ref[...]
    ms = jnp.mean(x * x, axis=-1, keepdims=True)
    o_ref[...] = ((x * lax.rsqrt(ms + EPS)) * g_ref[...]).astype(o_ref.dtype)


def rmsnorm(x, gains, layer, out_dtype, row0=0, n_rows=None, tm=1024):
    d = x.shape[1]
    m = x.shape[0] if n_rows is None else n_rows
    return pl.pallas_call(
        _rmsnorm_kernel,
        grid=(m // tm,),
        in_specs=[pl.BlockSpec((tm, d), lambda i: (row0 // tm + i, 0)),
                  pl.BlockSpec((None, 1, d), lambda i: (layer, 0, 0))],
        out_specs=pl.BlockSpec((tm, d), lambda i: (i, 0)),
        out_shape=jax.ShapeDtypeStruct((m, d), out_dtype),
        compiler_params=_params(1),
        name="rmsnorm",
    )(x, gains)


def _mm_kernel(*refs, n_w, has_res, scale, w_transposed, sub_rows):
    a_ref = refs[0]
    w_refs = refs[1:1 + n_w]
    res_ref = refs[1 + n_w] if has_res else None
    o_ref = refs[1 + n_w + int(has_res)]
    wb_refs = refs[2 + n_w + int(has_res):]

    @pl.when(pl.program_id(1) == 0)
    def _():
        for w_ref, wb_ref in zip(w_refs, wb_refs):
            wb_ref[...] = w_ref[...].astype(BF16)

    def mm(a, wb_ref):
        if w_transposed:
            return lax.dot_general(a, wb_ref[...], _NT, preferred_element_type=F32)
        return jnp.dot(a, wb_ref[...], preferred_element_type=F32)

    def rows_pass(rows):
        a = a_ref[rows, :]
        acc = mm(a, wb_refs[0])
        if n_w == 2:
            acc = jax.nn.silu(acc) * mm(a, wb_refs[1])
        if has_res:
            acc = res_ref[rows, :] + (acc if scale == 1.0 else scale * acc)
        o_ref[rows, :] = acc.astype(o_ref.dtype)

    tm = a_ref.shape[0]
    if tm <= sub_rows:
        rows_pass(slice(None))
    else:
        def body(r, carry):
            rows_pass(pl.ds(pl.multiple_of(r * sub_rows, sub_rows), sub_rows))
            return carry
        lax.fori_loop(0, tm // sub_rows, body, 0)


_MATMUL_ROW_TILES = ((2304, 1152), (1024, 1024), (512, 512), (256, 256))


def _matmul_tiles(m, k, n_cols, n_w, has_res, out_bytes):
    for tn in (1024, 512, 256, 128):
        if n_cols % tn:
            continue
        for tm, sub_rows in _MATMUL_ROW_TILES:
            if m % tm:
                continue
            need = (2 * tm * k * 2
                    + n_w * (2 * k * tn * 4 + k * tn * 2)
                    + 2 * tm * tn * out_bytes + (2 * tm * tn * 4 if has_res else 0)
                    + n_w * sub_rows * tn * 4)
            if need <= MATMUL_VMEM_BUDGET:
                return tn, tm, sub_rows
    raise ValueError("no matmul tiling fits VMEM")


def matmul(a, ws, layer, *, n_cols, out_dtype, res=None, scale=1.0, w_transposed=False):
    m, k = a.shape
    n_w = len(ws)
    tn, tm, sub_rows = _matmul_tiles(m, k, n_cols, n_w, res is not None, jnp.dtype(out_dtype).itemsize)
    in_specs = [pl.BlockSpec((tm, k), lambda j, i: (i, 0))]
    if w_transposed:
        in_specs += [pl.BlockSpec((None, tn, k), lambda j, i: (layer, j, 0)) for _ in ws]
    else:
        in_specs += [pl.BlockSpec((None, k, tn), lambda j, i: (layer, 0, j)) for _ in ws]
    args = [a, *ws]
    if res is not None:
        in_specs.append(pl.BlockSpec((tm, tn), lambda j, i: (i, j)))
        args.append(res)
    return pl.pallas_call(
        functools.partial(_mm_kernel, n_w=n_w, has_res=res is not None, scale=scale, w_transposed=w_transposed,
                          sub_rows=sub_rows),
        grid=(n_cols // tn, m // tm),
        in_specs=in_specs,
        out_specs=pl.BlockSpec((tm, tn), lambda j, i: (i, j)),
        out_shape=jax.ShapeDtypeStruct((m, n_cols), out_dtype),
        scratch_shapes=[pltpu.VMEM((tn, k) if w_transposed else (k, tn), BF16) for _ in ws],
        compiler_params=_params(2),
        name="matmul",
    )(*args)


_ROWS_K_CHUNK = 512


def _mm_rows_kernel(a_ref, w_ref, res_ref, g_ref, x_ref, h_ref, wb_ref, *, n_pro, scale):
    s = pl.program_id(0)

    @pl.when(s < n_pro)
    def _():
        start = pl.multiple_of(s * _ROWS_K_CHUNK, _ROWS_K_CHUNK)
        wb_ref[pl.ds(start, _ROWS_K_CHUNK), :] = w_ref[...].astype(BF16)

    @pl.when(s >= n_pro)
    def _():
        acc = jnp.dot(a_ref[...], wb_ref[...], preferred_element_type=F32)
        x = res_ref[...] + (acc if scale == 1.0 else scale * acc)
        x_ref[...] = x
        ms = jnp.mean(x * x, axis=-1, keepdims=True)
        h_ref[...] = ((x * lax.rsqrt(ms + EPS)) * g_ref[...]).astype(h_ref.dtype)


def matmul_residual_norm(a, w, layer, res, gains, gain_layer, scale=1.0):
    m, k = a.shape
    n = res.shape[1]
    n_pro = k // _ROWS_K_CHUNK
    fixed = k * n * 2 + 2 * _ROWS_K_CHUNK * n * 4
    for tm in (512, 256, 128):
        per_tile = 2 * tm * k * 2 + 2 * tm * n * (4 + 4 + 2) + tm * n * 4
        if m % tm == 0 and fixed + per_tile <= MATMUL_VMEM_BUDGET:
            break
    else:
        raise ValueError("no row tile fits VMEM")
    row = lambda s: (jnp.maximum(s - n_pro, 0), 0)
    return pl.pallas_call(
        functools.partial(_mm_rows_kernel, n_pro=n_pro, scale=scale),
        grid=(n_pro + m // tm,),
        in_specs=[pl.BlockSpec((tm, k), row),
                  pl.BlockSpec((None, _ROWS_K_CHUNK, n), lambda s: (layer, jnp.minimum(s, n_pro - 1), 0)),
                  pl.BlockSpec((tm, n), row),
                  pl.BlockSpec((None, 1, n), lambda s: (gain_layer, 0, 0))],
        out_specs=[pl.BlockSpec((tm, n), row), pl.BlockSpec((tm, n), row)],
        out_shape=[jax.ShapeDtypeStruct((m, n), F32), jax.ShapeDtypeStruct((m, n), BF16)],
        scratch_shapes=[pltpu.VMEM((k, n), BF16)],
        compiler_params=_params(1),
        name="matmul_residual_norm",
    )(a, w, res, gains)


S5_LANE_CHUNKS = S5_MODES // LANES
S5_POWERS = 8
S5_WIDTH = S5_GROUPS * S5_GROUP_CH
S5_SUPER = 4
S5_SUPER_CH = S5_WIDTH // S5_SUPER
S5_SUPER_MODES = S5_MODES // S5_SUPER
S5_SUPER_CHUNKS = S5_SUPER_MODES // LANES
S5_BLOCK_ROWS = 256
S5_TILE_TABLE = 4 * SUBLANES


def _cmul_add(xr, xi, ar, ai, br, bi, keep=None):
    pr = ar * br - ai * bi
    pi = ar * bi + ai * br
    if keep is not None:
        pr = jnp.where(keep, pr, 0.0)
        pi = jnp.where(keep, pi, 0.0)
    return xr + pr, xi + pi


def _s5_kernel(u_ref, h0r_ref, h0i_ref, pwr_ref, pwi_ref, lnr_ref, lni_ref, bsup_ref, csup_ref, d_ref, gw_ref,
               gb_ref, o_ref, hor_ref, hoi_ref, bur_ref, bui_ref, hs_ref, *, rb, chain):
    ntile = rb // SUBLANES
    u = u_ref[...]
    ub = u.astype(BF16)
    for q in range(S5_SUPER):
        bu = jnp.dot(ub[:, q * S5_SUPER_CH:(q + 1) * S5_SUPER_CH], bsup_ref[q], preferred_element_type=F32)
        for r in range(S5_SUPER_CHUNKS):
            bur_ref[q * S5_SUPER_CHUNKS + r] = bu[:, r * LANES:(r + 1) * LANES]
            bui_ref[q * S5_SUPER_CHUNKS + r] = bu[:, S5_SUPER_MODES + r * LANES:S5_SUPER_MODES + (r + 1) * LANES]

    if chain:
        @pl.when(pl.program_id(1) == 0)
        def _():
            hor_ref[...] = h0r_ref[...]
            hoi_ref[...] = h0i_ref[...]

    tile_idx = lax.broadcasted_iota(jnp.int32, (ntile, LANES), 0)
    tile_last = pl.ds(SUBLANES - 1, ntile, stride=SUBLANES)
    tile_steps = SUBLANES.bit_length() - 1

    def chunk(c, carry):
        pr = pwr_ref[c]
        pi = pwi_ref[c]
        xr = bur_ref[c].reshape(ntile, SUBLANES, LANES)
        xi = bui_ref[c].reshape(ntile, SUBLANES, LANES)
        for k in range(tile_steps):
            sh = 1 << k
            rows = slice((k + 1) * SUBLANES, (k + 2) * SUBLANES)
            xr, xi = _cmul_add(xr, xi, lnr_ref[c, rows, :], lni_ref[c, rows, :],
                               pltpu.roll(xr, sh, 1), pltpu.roll(xi, sh, 1))
        p8r, p8i = pr[tile_steps:tile_steps + 1], pi[tile_steps:tile_steps + 1]
        if chain:
            bur_ref[c] = xr.reshape(rb, LANES)
            bui_ref[c] = xi.reshape(rb, LANES)
            tr = bur_ref[c, tile_last, :]
            ti = bui_ref[c, tile_last, :]
            hr, hi = hor_ref[c], hoi_ref[c]
            first = tile_idx == 0
            cr, ci = _cmul_add(tr, ti, p8r, p8i, hr, hi, first)
            for j in range(ntile.bit_length() - 1):
                sh = 1 << j
                k = tile_steps + j
                cr, ci = _cmul_add(cr, ci, pr[k:k + 1], pi[k:k + 1], pltpu.roll(cr, sh, 0), pltpu.roll(ci, sh, 0),
                                   tile_idx >= sh)
            prev_r = jnp.where(first, hr, pltpu.roll(cr, 1, 0))
            prev_i = jnp.where(first, hi, pltpu.roll(ci, 1, 0))
            hor_ref[c] = cr[ntile - 1:ntile]
            hoi_ref[c] = ci[ntile - 1:ntile]
        else:
            prev_r, prev_i = h0r_ref[c], h0i_ref[c]
        lr = lnr_ref[c, 0:SUBLANES, :]
        li = lni_ref[c, 0:SUBLANES, :]
        for k in range(ntile):
            rows = slice(k * SUBLANES, (k + 1) * SUBLANES)
            yr, yi = _cmul_add(xr[k], xi[k], lr, li, prev_r[k:k + 1], prev_i[k:k + 1])
            bur_ref[c, rows, :] = yr
            bui_ref[c, rows, :] = yi
        if not chain:
            hor_ref[c] = bur_ref[c, tile_last, :]
            hoi_ref[c] = bui_ref[c, tile_last, :]
        col = pl.multiple_of((c // S5_SUPER_CHUNKS) * 2 * S5_SUPER_MODES + (c % S5_SUPER_CHUNKS) * LANES, LANES)
        hs_ref[:, pl.ds(col, LANES)] = bur_ref[c].astype(BF16)
        hs_ref[:, pl.ds(col + S5_SUPER_MODES, LANES)] = bui_ref[c].astype(BF16)
        return carry

    lax.fori_loop(0, S5_LANE_CHUNKS, chunk, 0)

    y = jnp.concatenate(
        [jnp.dot(hs_ref[:, q * 2 * S5_SUPER_MODES:(q + 1) * 2 * S5_SUPER_MODES], csup_ref[q],
                 preferred_element_type=F32) for q in range(S5_SUPER)], axis=1) + d_ref[...] * u
    y = jax.nn.gelu(y)
    gate = jax.nn.sigmoid(_bdot(y, gw_ref[...]) + gb_ref[...])
    o_ref[...] = (y * gate).astype(o_ref.dtype)


def s5_mixer(proj, mix, row0, n_seq, seq_len, h0r, h0i, tabs, layer):
    chain = seq_len > SUBLANES
    rb = S5_BLOCK_ROWS
    if chain:
        grid = (n_seq, seq_len // rb)
        rows = lambda s, b: (row0 // rb + s * (seq_len // rb) + b, 0)
        h0r = h0r.reshape(S5_LANE_CHUNKS, n_seq, 1, LANES)
        h0i = h0i.reshape(S5_LANE_CHUNKS, n_seq, 1, LANES)
        st_spec = pl.BlockSpec((S5_LANE_CHUNKS, None, 1, LANES), lambda s, b: (0, s, 0, 0))
        st_shape = jax.ShapeDtypeStruct((S5_LANE_CHUNKS, n_seq, 1, LANES), F32)
    else:
        assert seq_len == SUBLANES
        nseg = rb // seq_len
        grid = (n_seq // nseg, 1)
        rows = lambda s, b: (row0 // rb + s, 0)
        st_spec = pl.BlockSpec((S5_LANE_CHUNKS, nseg, LANES), lambda s, b: (0, s, 0))
        st_shape = jax.ShapeDtypeStruct((S5_LANE_CHUNKS, n_seq, LANES), F32)
    const3 = lambda s, b: (layer, 0, 0)
    const4 = lambda s, b: (layer, 0, 0, 0)
    pw_spec = pl.BlockSpec((None, S5_LANE_CHUNKS, S5_POWERS, LANES), const4)
    lin_spec = pl.BlockSpec((None, S5_LANE_CHUNKS, S5_TILE_TABLE, LANES), const4)
    mix, hr, hi = _pallas(
        functools.partial(_s5_kernel, rb=rb, chain=chain),
        grid=grid,
        in_specs=[pl.BlockSpec((rb, S5_WIDTH), rows), st_spec, st_spec, pw_spec, pw_spec, lin_spec, lin_spec,
                  pl.BlockSpec((None, S5_SUPER, S5_SUPER_CH, 2 * S5_SUPER_MODES), const4),
                  pl.BlockSpec((None, S5_SUPER, 2 * S5_SUPER_MODES, S5_SUPER_CH), const4),
                  pl.BlockSpec((None, 1, S5_WIDTH), const3),
                  pl.BlockSpec((None, S5_WIDTH, S5_WIDTH), const3),
                  pl.BlockSpec((None, 1, S5_WIDTH), const3)],
        args=[proj, h0r, h0i, tabs["pw_re"], tabs["pw_im"], tabs["lin_re"], tabs["lin_im"], tabs["bsup"],
              tabs["csup"], tabs["d"], tabs["glu_w"], tabs["glu_b"]],
        out_specs=[pl.BlockSpec((rb, S5_WIDTH), rows), st_spec, st_spec],
        out_shapes=[jax.ShapeDtypeStruct((proj.shape[0], MIX_WIDTH), BF16), st_shape, st_shape],
        carried=[mix],
        scratch=[pltpu.VMEM((S5_LANE_CHUNKS, rb, LANES), F32),
                 pltpu.VMEM((S5_LANE_CHUNKS, rb, LANES), F32),
                 pltpu.VMEM((rb, 2 * S5_MODES), BF16)],
        name="s5_mixer",
    )
    return mix, hr.reshape(S5_LANE_CHUNKS, n_seq, LANES), hi.reshape(S5_LANE_CHUNKS, n_seq, LANES)


def s5_tables(lam_re, lam_im, b_re, b_im, c_re, c_im, d, log_step, glu_w, glu_b):
    depth = lam_re.shape[0]
    step = jnp.exp(log_step.astype(F32))[..., None]
    lr, li = lam_re.astype(F32), lam_im.astype(F32)
    mag = jnp.exp(lr * step)
    lbr, lbi = mag * jnp.cos(li * step), mag * jnp.sin(li * step)
    den = lr * lr + li * li
    fr = ((lbr - 1.0) * lr + lbi * li) / den
    fi = (lbi * lr - (lbr - 1.0) * li) / den
    bbr = fr[..., None] * b_re - fi[..., None] * b_im
    bbi = fr[..., None] * b_im + fi[..., None] * b_re
    per = S5_GROUPS // S5_SUPER
    eye = jnp.eye(per, dtype=F32)

    def blk_in(t):
        t = t.reshape(depth, S5_SUPER, per, S5_STATE, S5_GROUP_CH)
        return jnp.einsum("dqgpc,gh->dqgchp", t, eye).reshape(depth, S5_SUPER, S5_SUPER_CH, S5_SUPER_MODES)

    def blk_out(t):
        t = t.reshape(depth, S5_SUPER, per, S5_GROUP_CH, S5_STATE)
        return jnp.einsum("dqgcp,gh->dqgphc", t, eye).reshape(depth, S5_SUPER, S5_SUPER_MODES, S5_SUPER_CH)

    bsup = jnp.concatenate([blk_in(bbr), blk_in(bbi)], axis=-1).astype(BF16)
    csup = jnp.concatenate([blk_out(c_re.astype(F32)), blk_out(-c_im.astype(F32))], axis=2).astype(BF16)
    pr, pi = lbr.reshape(depth, S5_MODES), lbi.reshape(depth, S5_MODES)
    prs, pis = [], []
    for _ in range(S5_POWERS):
        prs.append(pr)
        pis.append(pi)
        pr, pi = pr * pr - pi * pi, 2.0 * pr * pi
    lr, li = prs[0], pis[0]
    lrs, lis = [], []
    for _ in range(SUBLANES):
        lrs.append(lr)
        lis.append(li)
        lr, li = lr * prs[0] - li * pis[0], lr * pis[0] + li * prs[0]
    for k in range(SUBLANES.bit_length() - 1):
        for t in range(SUBLANES):
            lrs.append(prs[k] if t >= (1 << k) else jnp.zeros_like(prs[k]))
            lis.append(pis[k] if t >= (1 << k) else jnp.zeros_like(pis[k]))

    def chunked(ts):
        t = jnp.stack(ts, axis=1).reshape(depth, len(ts), S5_LANE_CHUNKS, LANES)
        return t.transpose(0, 2, 1, 3)

    return dict(pw_re=chunked(prs), pw_im=chunked(pis), lin_re=chunked(lrs), lin_im=chunked(lis),
                bsup=bsup, csup=csup, d=d.astype(F32).reshape(depth, 1, S5_WIDTH), glu_w=glu_w.astype(BF16),
                glu_b=glu_b.astype(F32).reshape(depth, 1, S5_WIDTH))


def _to_chunk_major(h):
    n = h.shape[0]
    return h.reshape(n, S5_LANE_CHUNKS, LANES).transpose(1, 0, 2)


def _from_chunk_major(h):
    n = h.shape[1]
    return h.transpose(1, 0, 2).reshape(n, S5_GROUPS, S5_STATE)


_RET_LOG_GAMMA = tuple(math.log(1.0 - 2.0 ** (-5.0 - h)) for h in range(RET_HEADS))
RET_WIDTH = RET_HEADS * RET_HEAD_DIM


def _ret_kernel(q_ref, k_ref, v_ref, g_ref, cos_ref, sin_ref, nw_ref, s0_ref, o_ref, s_ref, y_ref, *, t, nb):
    @pl.when(pl.program_id(1) == 0)
    def _():
        s_ref[...] = s0_ref[...]

    cos = cos_ref[...]
    sin = sin_ref[...]
    ii = lax.broadcasted_iota(jnp.int32, (t, t), 0)
    jj = lax.broadcasted_iota(jnp.int32, (t, t), 1)
    ti = lax.broadcasted_iota(jnp.int32, (t, 1), 0).astype(F32)
    half = RET_HEAD_DIM // 2
    for h in range(RET_HEADS):
        lg = _RET_LOG_GAMMA[h]
        cols = slice(h * RET_HEAD_DIM, (h + 1) * RET_HEAD_DIM)
        decay = jnp.exp(jnp.where(ii >= jj, (ii - jj).astype(F32) * lg, -jnp.inf))
        grow = jnp.exp((ti + 1.0) * lg)
        tail = jnp.exp((t - 1.0 - ti) * lg)
        for n in range(nb):
            rows = slice(n * t, (n + 1) * t)
            q = q_ref[rows, cols]
            k = k_ref[rows, cols]
            v = v_ref[rows, cols].astype(BF16)
            q = q * cos + pltpu.roll(q, half, 1) * sin
            k = (k * cos + pltpu.roll(k, half, 1) * sin) * (RET_HEAD_DIM ** -0.5)
            s_prev = s_ref[n, h]
            scores = lax.dot_general(q.astype(BF16), k.astype(BF16), _NT, preferred_element_type=F32) * decay
            y = _bdot(scores, v) + _bdot(q * grow, s_prev)
            s_ref[n, h] = math.exp(t * lg) * s_prev + lax.dot_general(
                (k * tail).astype(BF16), v, _TN, preferred_element_type=F32)
            y = y * lax.rsqrt(jnp.mean(y * y, axis=-1, keepdims=True) + EPS)
            y = y * nw_ref[:, cols]
            y_ref[rows, cols] = jax.nn.silu(g_ref[rows, cols]) * y
    o_ref[...] = y_ref[...].astype(o_ref.dtype)


def retention_mixer(proj, mix, row0, n_seq, seq_len, s0, s0_layer, s_out, depth, cos2, sin2, norm_w, layer):
    t = math.gcd(seq_len, CHUNK)
    n_chunks = seq_len // t
    nb = 1 if n_chunks > 1 else 8
    rb = nb * t

    def col(cb):
        return pl.BlockSpec((rb, RET_WIDTH), lambda s, c: (row0 // rb + s * n_chunks + c, cb))

    st_block = (None, nb, RET_HEADS, RET_HEAD_DIM, RET_HEAD_DIM)
    tab_spec = pl.BlockSpec((t, RET_HEAD_DIM), lambda s, c: (c, 0))
    return _pallas(
        functools.partial(_ret_kernel, t=t, nb=nb),
        grid=(n_seq // nb, n_chunks),
        in_specs=[col(1), col(2), col(3), col(4), tab_spec, tab_spec,
                  pl.BlockSpec((None, 1, RET_WIDTH), lambda s, c: (layer, 0, 0)),
                  pl.BlockSpec(st_block, lambda s, c: (s0_layer, s, 0, 0, 0))],
        args=[proj, proj, proj, proj, cos2, sin2, norm_w, s0],
        out_specs=[col(1), pl.BlockSpec(st_block, lambda s, c: (layer, s, 0, 0, 0))],
        out_shapes=[jax.ShapeDtypeStruct((proj.shape[0], MIX_WIDTH), BF16),
                    jax.ShapeDtypeStruct((depth,) + s0.shape[1:], F32)],
        carried=[mix, s_out],
        scratch=[pltpu.VMEM((rb, RET_WIDTH), F32)],
        name="retention_mixer",
    )


def rope_tables(pos):
    half = RET_HEAD_DIM // 2
    inv = ROPE_BASE ** (-jnp.arange(half, dtype=F32) / half)
    ang = pos.astype(F32)[:, None] * inv[None, :]
    cos, sin = jnp.cos(ang), jnp.sin(ang)
    return jnp.concatenate([cos, cos], axis=-1), jnp.concatenate([-sin, sin], axis=-1)


SSD_WIDTH = SSD_HEADS * SSD_HEAD_DIM
SSD_BC = SSD_GROUPS * SSD_STATE
SSD_CONV_DIM = SSD_WIDTH + 2 * SSD_BC
_SSD_PREV = SUBLANES
_COL_BLOCK = 512
_SSD_Z_COL = 2560
_SSD_XBC_COL = _SSD_Z_COL + SSD_WIDTH


def _ssd_kernel(z0_ref, z1_ref, x0_ref, x1_ref, x2_ref, dt_ref, cs_ref, cw_ref, cb_ref, dtb_ref, aneg_ref, dsk_ref,
                nw_ref, s0_ref, o_ref, s_ref, co_ref, ext_ref, xbc_ref, y_ref, xdt_ref, xtl_ref, fs_ref, *,
                t, nb, conv_rows):
    keep = SSD_CONV - 1
    rep = SSD_HEADS // SSD_GROUPS
    grp = rep * SSD_HEAD_DIM

    def conv_row(n):
        return pl.ds(n, 1) if conv_rows == nb else pl.ds(pl.program_id(0) * nb + n, 1)

    @pl.when(pl.program_id(1) == 0)
    def _():
        s_ref[...] = s0_ref[...]
        for n in range(nb):
            for r in range(keep):
                ext_ref[n, _SSD_PREV - keep + r:_SSD_PREV - keep + r + 1, :] = cs_ref[r, conv_row(n), :]

    rb = nb * t
    ii = lax.broadcasted_iota(jnp.int32, (rb, rb), 0)
    jj = lax.broadcasted_iota(jnp.int32, (rb, rb), 1)
    same_seq = (ii // t) == (jj // t)
    causal = jnp.logical_and(same_seq, ii >= jj)
    tri = causal.astype(BF16)
    seq_ones = same_seq.astype(BF16)
    cw = cw_ref[...]
    for n in range(nb):
        rows = slice(n * t, (n + 1) * t)
        ext_ref[n, _SSD_PREV:_SSD_PREV + t, 0:_COL_BLOCK] = x0_ref[rows, :]
        ext_ref[n, _SSD_PREV:_SSD_PREV + t, _COL_BLOCK:2 * _COL_BLOCK] = x1_ref[rows, :]
        ext_ref[n, _SSD_PREV:_SSD_PREV + t, 2 * _COL_BLOCK:] = x2_ref[rows, :]
        conv = cb_ref[...] + cw[keep:keep + 1] * ext_ref[n, _SSD_PREV:_SSD_PREV + t, :]
        for back in range(1, SSD_CONV):
            conv = conv + cw[keep - back:keep - back + 1] * ext_ref[n, pl.ds(_SSD_PREV - back, t), :]
        tail_rows = ext_ref[n, _SSD_PREV + t - keep:_SSD_PREV + t, :]
        for r in range(keep):
            co_ref[r, conv_row(n), :] = tail_rows[r:r + 1]
        ext_ref[n, _SSD_PREV - keep:_SSD_PREV, :] = tail_rows
        xbc_ref[rows, :] = jax.nn.silu(conv)

    dt = jax.nn.softplus(dt_ref[...] + dtb_ref[...])
    la = dt * aneg_ref[...]
    sums = jnp.dot(jnp.concatenate([jnp.concatenate([tri] * 3, axis=1), jnp.concatenate([seq_ones] * 3, axis=1)],
                                   axis=0),
                   jnp.concatenate(_split3(la), axis=0), preferred_element_type=F32)
    cum, tot = sums[:rb], sums[rb:]
    cum_t = cum.T
    e_last = jnp.exp(tot)

    spread = (lax.broadcasted_iota(jnp.int32, (3 * LANES, SSD_WIDTH), 0) % LANES
              == lax.broadcasted_iota(jnp.int32, (3 * LANES, SSD_WIDTH), 1) // SSD_HEAD_DIM).astype(BF16)
    factors = jnp.concatenate([dt, dt * jnp.exp(tot - cum), jnp.exp(cum)], axis=0)
    cols = jnp.dot(jnp.concatenate(_split3(factors), axis=1), spread, preferred_element_type=F32)
    xh = xbc_ref[:, :SSD_WIDTH]
    xdt_ref[...] = xh * cols[:rb]
    xtl_ref[...] = xh * cols[rb:2 * rb]
    e_cum_cols = cols[2 * rb:]
    first_of_pair = lax.broadcasted_iota(jnp.int32, (1, 2 * SSD_HEAD_DIM), 1) < SSD_HEAD_DIM
    for g in range(SSD_GROUPS):
        gc = slice(g * grp, (g + 1) * grp)
        bg = xbc_ref[:, SSD_WIDTH + g * SSD_STATE:SSD_WIDTH + (g + 1) * SSD_STATE]
        cg = xbc_ref[:, SSD_WIDTH + SSD_BC + g * SSD_STATE:SSD_WIDTH + SSD_BC + (g + 1) * SSD_STATE]
        gram = lax.dot_general(cg.astype(BF16), bg.astype(BF16), _NT, preferred_element_type=F32)
        for n in range(nb):
            rows = slice(n * t, (n + 1) * t)
            s_grp = s_ref[n, g * rep:(g + 1) * rep].reshape(grp, SSD_STATE)
            fs_ref[rows, gc] = lax.dot_general(cg[rows].astype(BF16), s_grp.astype(BF16), _NT,
                                               preferred_element_type=F32)
            s_add = lax.dot_general(xtl_ref[rows, gc].astype(BF16), bg[rows].astype(BF16), _TN,
                                    preferred_element_type=F32)
            for hh in range(rep):
                h = g * rep + hh
                s_ref[n, h] = (e_last[n * t:n * t + 1, h:h + 1] * s_ref[n, h]
                               + s_add[hh * SSD_HEAD_DIM:(hh + 1) * SSD_HEAD_DIM, :])
        for pair in range(rep // 2):
            h = g * rep + 2 * pair
            pc = slice(h * SSD_HEAD_DIM, (h + 2) * SSD_HEAD_DIM)
            xp = xdt_ref[:, pc]
            scores = [gram * jnp.exp(jnp.where(causal, cum[:, k:k + 1] - cum_t[k:k + 1, :], -jnp.inf))
                      for k in (h, h + 1)]
            y_ref[:, pc] = _bdot(jnp.concatenate(scores, axis=1),
                                 jnp.concatenate([jnp.where(first_of_pair, xp, 0.0),
                                                  jnp.where(first_of_pair, 0.0, xp)], axis=0))
    y = y_ref[...] + e_cum_cols * fs_ref[...] + xh * dsk_ref[...]
    y = jnp.concatenate([y[:, :_COL_BLOCK] * jax.nn.silu(z0_ref[...]),
                         y[:, _COL_BLOCK:] * jax.nn.silu(z1_ref[...])], axis=1)
    y = y * lax.rsqrt(jnp.mean(y * y, axis=-1, keepdims=True) + EPS)
    o_ref[...] = (y * nw_ref[...]).astype(o_ref.dtype)


def ssd_mixer(proj, dt_raw, mix, row0, n_seq, seq_len, s0, conv0, st_layer, s_out, conv_out, depth, tabs, layer):
    t = math.gcd(seq_len, CHUNK)
    n_chunks = seq_len // t
    nb = 1 if n_chunks > 1 else SUBLANES
    rb = nb * t
    conv_rows = nb if nb % SUBLANES == 0 else n_seq

    def col(cb, width=_COL_BLOCK):
        return pl.BlockSpec((rb, width), lambda s, c: (row0 // rb + s * n_chunks + c, cb))

    def conv_idx(lyr):
        return lambda s, c: (lyr, 0, s if conv_rows == nb else 0, 0)

    z_cb = _SSD_Z_COL // _COL_BLOCK
    x_cb = _SSD_XBC_COL // _COL_BLOCK
    const3 = lambda s, c: (layer, 0, 0)
    st_block = (None, nb, SSD_HEADS, SSD_HEAD_DIM, SSD_STATE)
    cs_block = (None, SSD_CONV - 1, conv_rows, SSD_CONV_DIM)
    return _pallas(
        functools.partial(_ssd_kernel, t=t, nb=nb, conv_rows=conv_rows),
        grid=(n_seq // nb, n_chunks),
        in_specs=[col(z_cb), col(z_cb + 1), col(x_cb), col(x_cb + 1), col(x_cb + 2), col(0, LANES),
                  pl.BlockSpec(cs_block, conv_idx(st_layer)),
                  pl.BlockSpec((None, SSD_CONV, SSD_CONV_DIM), const3),
                  pl.BlockSpec((None, 1, SSD_CONV_DIM), const3),
                  pl.BlockSpec((None, 1, LANES), const3),
                  pl.BlockSpec((None, 1, LANES), const3),
                  pl.BlockSpec((None, 1, SSD_WIDTH), const3),
                  pl.BlockSpec((None, 1, SSD_WIDTH), const3),
                  pl.BlockSpec(st_block, lambda s, c: (st_layer, s, 0, 0, 0))],
        args=[proj, proj, proj, proj, proj, dt_raw, conv0, tabs["conv_w"], tabs["conv_b"], tabs["dt_bias"],
              tabs["a_neg"], tabs["d"], tabs["norm"], s0],
        out_specs=[col(1, SSD_WIDTH),
                   pl.BlockSpec(st_block, lambda s, c: (layer, s, 0, 0, 0)),
                   pl.BlockSpec(cs_block, conv_idx(layer))],
        out_shapes=[jax.ShapeDtypeStruct((proj.shape[0], MIX_WIDTH), BF16),
                    jax.ShapeDtypeStruct((depth,) + s0.shape[1:], F32),
                    jax.ShapeDtypeStruct((depth,) + conv0.shape[1:], F32)],
        carried=[mix, s_out, conv_out],
        scratch=[pltpu.VMEM((nb, _SSD_PREV + t, SSD_CONV_DIM), F32),
                 pltpu.VMEM((rb, SSD_CONV_DIM), F32)] + [pltpu.VMEM((rb, SSD_WIDTH), F32)] * 4,
        name="ssd_mixer",
    )


def _softmax(s):
    s = s - jnp.max(s, axis=-1, keepdims=True)
    p = jnp.exp(s)
    return p / jnp.sum(p, axis=-1, keepdims=True)


def _xattn_kernel(q_ref, k_ref, v_ref, o_ref, *, tq, nb, head_dim, heads_axis):
    if not heads_axis:
        for h in range(XATTN_HEADS):
            cols = slice(h * head_dim, (h + 1) * head_dim)
            q = q_ref[:, cols]
            pieces = []
            for n in range(nb):
                s = lax.dot_general(q, k_ref[n, :, cols].astype(BF16), _NT,
                                    preferred_element_type=F32)[n * tq:(n + 1) * tq]
                pieces.append(_bdot(_softmax(s * (head_dim ** -0.5)), v_ref[n, :, cols]))
            att = pieces[0] if nb == 1 else jnp.concatenate(pieces, axis=0)
            o_ref[:, cols] = att.astype(o_ref.dtype)
        return

    flat = MEM_TOKENS * XATTN_HEADS
    rows = XATTN_HEADS * tq
    row_head = lax.broadcasted_iota(jnp.int32, (rows, flat), 0) // tq
    col_head = lax.broadcasted_iota(jnp.int32, (rows, flat), 1) % XATTN_HEADS
    same_head = row_head == col_head
    q_all = q_ref[...].astype(F32)
    outs = []
    for n in range(nb):
        q = jnp.concatenate([q_all[n * tq:(n + 1) * tq, h * head_dim:(h + 1) * head_dim]
                             for h in range(XATTN_HEADS)], axis=0)
        k = k_ref[n].reshape(flat, head_dim)
        v = v_ref[n].reshape(flat, head_dim)
        s = lax.dot_general(q.astype(BF16), k.astype(BF16), _NT, preferred_element_type=F32)
        s = jnp.where(same_head, s * (head_dim ** -0.5), -jnp.inf)
        outs.append(_bdot(_softmax(s), v))
    for h in range(XATTN_HEADS):
        att = jnp.concatenate([o[h * tq:(h + 1) * tq] for o in outs], axis=0)
        o_ref[:, h * head_dim:(h + 1) * head_dim] = att.astype(o_ref.dtype)


def cross_attention(q, att, row0, n_seq, seq_len, mem_k, mem_v, layer=None):
    d = q.shape[-1]
    head_dim = d // XATTN_HEADS
    if seq_len > SUBLANES:
        tq, nb = 512, 1
    else:
        tq, nb = seq_len, 4
    n_q = seq_len // tq
    rb = nb * tq
    if layer is None:
        kv_spec = pl.BlockSpec((nb, MEM_TOKENS, d), lambda s, i: (s, 0, 0))
    else:
        kv_spec = pl.BlockSpec((None, nb, MEM_TOKENS, XATTN_HEADS, head_dim), lambda s, i: (layer, s, 0, 0, 0))
    q_spec = pl.BlockSpec((rb, d), lambda s, i: (row0 // rb + s * n_q + i, 0))
    (att,) = _pallas(
        functools.partial(_xattn_kernel, tq=tq, nb=nb, head_dim=head_dim, heads_axis=layer is not None),
        grid=(n_seq // nb, n_q),
        in_specs=[q_spec, kv_spec, kv_spec],
        args=[q, mem_k, mem_v],
        out_specs=[q_spec],
        out_shapes=[jax.ShapeDtypeStruct(q.shape, BF16)],
        carried=[att],
        name="cross_attention",
    )
    return att


def kernel(x_prompt, x_sample, mem_prompt, state_s5_re, state_s5_im, state_ret, state_ssm, state_conv, cache_mem_k, cache_mem_v, ffn1_norm, ffn1_w1, ffn1_w3, ffn1_w2, mix_norm, w_in, w_out, s5_lambda_re, s5_lambda_im, s5_b_re, s5_b_im, s5_c_re, s5_c_im, s5_d, s5_log_step, s5_glu_w, s5_glu_b, ret_norm, ssd_conv_w, ssd_conv_b, ssd_dt_bias, ssd_a_log, ssd_d, ssd_norm, xattn_norm, xattn_wq, xattn_wk, xattn_wv, xattn_wo, ffn2_norm, ffn2_w1, ffn2_w3, ffn2_w2, final_norm):
    bp, lp, d = x_prompt.shape
    bs, ls, _ = x_sample.shape
    depth = w_in.shape[0]
    ffn = ffn1_w1.shape[-1]
    n_p, n_s = bp * lp, bs * ls

    def gain(g):
        return g.astype(F32).reshape(-1, 1, g.shape[-1])

    def lane_pad(v):
        return jnp.pad(v.astype(F32), ((0, 0), (0, LANES - v.shape[-1]))).reshape(depth, 1, LANES)

    s5_tabs = s5_tables(s5_lambda_re, s5_lambda_im, s5_b_re, s5_b_im, s5_c_re, s5_c_im, s5_d, s5_log_step,
                        s5_glu_w, s5_glu_b)
    main_cols = w_in.shape[-1] - SSD_HEADS
    w_dt = jnp.pad(w_in[:, :, main_cols:], ((0, 0), (0, 0), (0, LANES - SSD_HEADS)))
    w_in_t = w_in.swapaxes(1, 2)
    ssd_tabs = dict(conv_w=ssd_conv_w.astype(F32), conv_b=gain(ssd_conv_b), dt_bias=lane_pad(ssd_dt_bias),
                    a_neg=lane_pad(-jnp.exp(ssd_a_log.astype(F32))),
                    d=gain(jnp.repeat(ssd_d, SSD_HEAD_DIM, axis=-1)), norm=gain(ssd_norm))
    ret_w = gain(ret_norm)
    cos_p, sin_p = rope_tables(jnp.arange(lp, dtype=jnp.int32))
    cos_s, sin_s = rope_tables(PAST_LEN + jnp.arange(ls, dtype=jnp.int32))
    mem = mem_prompt.reshape(bp * MEM_TOKENS, d).astype(BF16)
    norms = {name: gain(g) for name, g in dict(ffn1=ffn1_norm, mix=mix_norm, xattn=xattn_norm, ffn2=ffn2_norm,
                                                final=final_norm).items()}

    ssm_t = state_ssm.swapaxes(-1, -2)
    conv_t = state_conv.swapaxes(1, 2)
    s5_re_cm = jnp.stack([_to_chunk_major(state_s5_re[l]) for l in range(depth)])
    s5_im_cm = jnp.stack([_to_chunk_major(state_s5_im[l]) for l in range(depth)])
    zeros_s5 = jnp.zeros((S5_LANE_CHUNKS, bp, LANES), F32)
    zeros_ret = jnp.zeros((1, bp) + state_ret.shape[2:], F32)
    zeros_ssm = jnp.zeros((1, bp) + ssm_t.shape[2:], F32)
    zeros_conv = jnp.zeros((1, SSD_CONV - 1, bp, SSD_CONV_DIM), F32)

    x = jnp.concatenate([x_prompt.reshape(n_p, d), x_sample.reshape(n_s, d)], axis=0)
    lists = {name: [] for name in ("p_s5re", "p_s5im", "p_mk", "p_mv", "s_s5re", "s_s5im")}
    p_ret = jnp.zeros((depth, bp) + state_ret.shape[2:], F32)
    s_ret = jnp.zeros(state_ret.shape, F32)
    p_ssm = jnp.zeros((depth, bp) + ssm_t.shape[2:], F32)
    s_ssm = jnp.zeros(ssm_t.shape, F32)
    p_conv = jnp.zeros((depth, SSD_CONV - 1, bp, SSD_CONV_DIM), F32)
    s_conv = jnp.zeros(conv_t.shape, F32)

    h = rmsnorm(x, norms["ffn1"], 0, BF16)
    assert MIX_WIDTH == d
    spare = h
    for l in range(depth):
        a = matmul(h, [ffn1_w1, ffn1_w3], l, n_cols=ffn, out_dtype=BF16)
        x, h = matmul_residual_norm(a, ffn1_w2, l, x, norms["mix"], l, scale=0.5)

        proj = matmul(h, [w_in_t], l, n_cols=main_cols, out_dtype=F32, w_transposed=True)
        dt_raw = matmul(h, [w_dt], l, n_cols=LANES, out_dtype=F32)

        mix, hr_p, hi_p = s5_mixer(proj, spare, 0, bp, lp, zeros_s5, zeros_s5, s5_tabs, l)
        mix, hr_s, hi_s = s5_mixer(proj, mix, n_p, bs, ls, s5_re_cm[l], s5_im_cm[l], s5_tabs, l)
        mix, p_ret = retention_mixer(proj, mix, 0, bp, lp, zeros_ret, 0, p_ret, depth, cos_p, sin_p, ret_w, l)
        mix, s_ret = retention_mixer(proj, mix, n_p, bs, ls, state_ret, l, s_ret, depth, cos_s, sin_s, ret_w, l)
        mix, p_ssm, p_conv = ssd_mixer(proj, dt_raw, mix, 0, bp, lp, zeros_ssm, zeros_conv, 0, p_ssm, p_conv,
                                       depth, ssd_tabs, l)
        mix, s_ssm, s_conv = ssd_mixer(proj, dt_raw, mix, n_p, bs, ls, ssm_t, conv_t, l, s_ssm, s_conv,
                                       depth, ssd_tabs, l)
        x, h = matmul_residual_norm(mix, w_out, l, x, norms["xattn"], l)
        q = matmul(h, [xattn_wq], l, n_cols=d, out_dtype=BF16)
        mk = matmul(mem, [xattn_wk], l, n_cols=d, out_dtype=F32)
        mv = matmul(mem, [xattn_wv], l, n_cols=d, out_dtype=F32)
        att = cross_attention(q, mix, 0, bp, lp, mk.astype(BF16).reshape(bp, MEM_TOKENS, d),
                              mv.astype(BF16).reshape(bp, MEM_TOKENS, d))
        att = cross_attention(q, att, n_p, bs, ls, cache_mem_k, cache_mem_v, layer=l)
        x, h = matmul_residual_norm(att, xattn_wo, l, x, norms["ffn2"], l)
        spare = att
        a = matmul(h, [ffn2_w1, ffn2_w3], l, n_cols=ffn, out_dtype=BF16)
        if l + 1 < depth:
            x, h = matmul_residual_norm(a, ffn2_w2, l, x, norms["ffn1"], l + 1, scale=0.5)
        else:
            x = matmul(a, [ffn2_w2], l, n_cols=d, out_dtype=F32, res=x, scale=0.5)

        lists["p_s5re"].append(_from_chunk_major(hr_p))
        lists["p_s5im"].append(_from_chunk_major(hi_p))
        lists["p_mk"].append(mk.reshape(bp, MEM_TOKENS, XATTN_HEADS, d // XATTN_HEADS))
        lists["p_mv"].append(mv.reshape(bp, MEM_TOKENS, XATTN_HEADS, d // XATTN_HEADS))
        lists["s_s5re"].append(_from_chunk_major(hr_s))
        lists["s_s5im"].append(_from_chunk_major(hi_s))

    y_p = rmsnorm(x, norms["final"], 0, F32, 0, n_p)
    y_s = rmsnorm(x, norms["final"], 0, F32, n_p, n_s)
    st = {name: jnp.stack(v) for name, v in lists.items()}
    return (y_p.reshape(bp, lp, d), y_s.reshape(bs, ls, d),
            st["p_s5re"], st["p_s5im"], p_ret, p_ssm.swapaxes(-1, -2), p_conv.swapaxes(1, 2),
            st["p_mk"], st["p_mv"],
            st["s_s5re"], st["s_s5im"], s_ret, s_ssm.swapaxes(-1, -2), s_conv.swapaxes(1, 2))
```

```python
import functools
import math

import jax
import jax.numpy as jnp
from jax import lax
from jax.experimental import pallas as pl
from jax.experimental.pallas import tpu as pltpu

F32 = jnp.float32
BF16 = jnp.bfloat16

EPS = 1e-6
S5_GROUPS = 32
S5_GROUP_CH = 16
S5_STATE = 64
S5_MODES = S5_GROUPS * S5_STATE
RET_HEADS = 4
RET_HEAD_DIM = 128
ROPE_BASE = 10000.0
SSD_HEADS = 16
SSD_HEAD_DIM = 64
SSD_GROUPS = 2
SSD_STATE = 128
SSD_CONV = 4
MEM_TOKENS = 256
XATTN_HEADS = 4
CHUNK = 128
PAST_LEN = 16384

MIX_WIDTH = 2048

LANES = 128
SUBLANES = 8
MIB = 1024 * 1024
VMEM_LIMIT_BYTES = 60 * MIB
MATMUL_VMEM_BUDGET = VMEM_LIMIT_BYTES - 10 * MIB

_NT = (((1,), (1,)), ((), ()))
_TN = (((0,), (0,)), ((), ()))


def _params(n_axes):
    return pltpu.CompilerParams(dimension_semantics=("arbitrary",) * n_axes,
                                vmem_limit_bytes=VMEM_LIMIT_BYTES)


def _pallas(body, *, grid, in_specs, args, out_specs, out_shapes, carried=(), scratch=(), name):
    carried = list(carried) + [None] * (len(out_shapes) - len(carried))
    extra = [(i, c) for i, c in enumerate(carried) if c is not None]
    n_in = len(args)

    def with_carried(*refs):
        body(*refs[:n_in], *refs[n_in + len(extra):])

    return pl.pallas_call(
        with_carried,
        grid=grid,
        in_specs=list(in_specs) + [pl.BlockSpec(memory_space=pl.ANY)] * len(extra),
        out_specs=out_specs,
        out_shape=out_shapes,
        input_output_aliases={n_in + k: i for k, (i, _) in enumerate(extra)},
        scratch_shapes=list(scratch),
        compiler_params=_params(len(grid)),
        name=name,
    )(*args, *[c for _, c in extra])


def _bdot(a, b):
    return jnp.dot(a.astype(BF16), b.astype(BF16), preferred_element_type=F32)


def _split3(x):
    x1 = x.astype(BF16)
    r1 = x - x1.astype(F32)
    x2 = r1.astype(BF16)
    x3 = (r1 - x2.astype(F32)).astype(BF16)
    return x1, x2, x3


def _rmsnorm_kernel(x_ref, g_ref, o_ref):
    x = x_ref[...]
    ms = jnp.mean(x * x, axis=-1, keepdims=True)
    o_ref[...] = ((x * lax.rsqrt(ms + EPS)) * g_ref[...]).astype(o_ref.dtype)


def rmsnorm(x, gains, layer, out_dtype, row0=0, n_rows=None, tm=1024):
    d = x.shape[1]
    m = x.shape[0] if n_rows is None else n_rows
    return pl.pallas_call(
        _rmsnorm_kernel,
        grid=(m // tm,),
        in_specs=[pl.BlockSpec((tm, d), lambda i: (row0 // tm + i, 0)),
                  pl.BlockSpec((None, 1, d), lambda i: (layer, 0, 0))],
        out_specs=pl.BlockSpec((tm, d), lambda i: (i, 0)),
        out_shape=jax.ShapeDtypeStruct((m, d), out_dtype),
        compiler_params=_params(1),
        name="rmsnorm",
    )(x, gains)


def _mm_kernel(*refs, n_w, has_res, scale, w_transposed, sub_rows):
    a_ref = refs[0]
    w_refs = refs[1:1 + n_w]
    res_ref = refs[1 + n_w] if has_res else None
    o_ref = refs[1 + n_w + int(has_res)]
    wb_refs = refs[2 + n_w + int(has_res):]

    @pl.when(pl.program_id(1) == 0)
    def _():
        for w_ref, wb_ref in zip(w_refs, wb_refs):
            wb_ref[...] = w_ref[...].astype(BF16)

    def mm(a, wb_ref):
        if w_transposed:
            return lax.dot_general(a, wb_ref[...], _NT, preferred_element_type=F32)
        return jnp.dot(a, wb_ref[...], preferred_element_type=F32)

    def rows_pass(rows):
        a = a_ref[rows, :]
        acc = mm(a, wb_refs[0])
        if n_w == 2:
            acc = jax.nn.silu(acc) * mm(a, wb_refs[1])
        if has_res:
            acc = res_ref[rows, :] + (acc if scale == 1.0 else scale * acc)
        o_ref[rows, :] = acc.astype(o_ref.dtype)

    tm = a_ref.shape[0]
    if tm <= sub_rows:
        rows_pass(slice(None))
    else:
        def body(r, carry):
            rows_pass(pl.ds(pl.multiple_of(r * sub_rows, sub_rows), sub_rows))
            return carry
        lax.fori_loop(0, tm // sub_rows, body, 0)


_MATMUL_ROW_TILES = ((2304, 1152), (1024, 1024), (512, 512), (256, 256))


def _matmul_tiles(m, k, n_cols, n_w, has_res, out_bytes):
    for tn in (1024, 512, 256, 128):
        if n_cols % tn:
            continue
        for tm, sub_rows in _MATMUL_ROW_TILES:
            if m % tm:
                continue
            need = (2 * tm * k * 2
                    + n_w * (2 * k * tn * 4 + k * tn * 2)
                    + 2 * tm * tn * out_bytes + (2 * tm * tn * 4 if has_res else 0)
                    + n_w * sub_rows * tn * 4)
            if need <= MATMUL_VMEM_BUDGET:
                return tn, tm, sub_rows
    raise ValueError("no matmul tiling fits VMEM")


def matmul(a, ws, layer, *, n_cols, out_dtype, res=None, scale=1.0, w_transposed=False):
    m, k = a.shape
    n_w = len(ws)
    tn, tm, sub_rows = _matmul_tiles(m, k, n_cols, n_w, res is not None, jnp.dtype(out_dtype).itemsize)
    in_specs = [pl.BlockSpec((tm, k), lambda j, i: (i, 0))]
    if w_transposed:
        in_specs += [pl.BlockSpec((None, tn, k), lambda j, i: (layer, j, 0)) for _ in ws]
    else:
        in_specs += [pl.BlockSpec((None, k, tn), lambda j, i: (layer, 0, j)) for _ in ws]
    args = [a, *ws]
    if res is not None:
        in_specs.append(pl.BlockSpec((tm, tn), lambda j, i: (i, j)))
        args.append(res)
    return pl.pallas_call(
        functools.partial(_mm_kernel, n_w=n_w, has_res=res is not None, scale=scale, w_transposed=w_transposed,
                          sub_rows=sub_rows),
        grid=(n_cols // tn, m // tm),
        in_specs=in_specs,
        out_specs=pl.BlockSpec((tm, tn), lambda j, i: (i, j)),
        out_shape=jax.ShapeDtypeStruct((m, n_cols), out_dtype),
        scratch_shapes=[pltpu.VMEM((tn, k) if w_transposed else (k, tn), BF16) for _ in ws],
        compiler_params=_params(2),
        name="matmul",
    )(*args)


_ROWS_K_CHUNK = 512


def _mm_rows_kernel(a_ref, w_ref, res_ref, g_ref, x_ref, h_ref, wb_ref, *, n_pro, scale):
    s = pl.program_id(0)

    @pl.when(s < n_pro)
    def _():
        start = pl.multiple_of(s * _ROWS_K_CHUNK, _ROWS_K_CHUNK)
        wb_ref[pl.ds(start, _ROWS_K_CHUNK), :] = w_ref[...].astype(BF16)

    @pl.when(s >= n_pro)
    def _():
        acc = jnp.dot(a_ref[...], wb_ref[...], preferred_element_type=F32)
        x = res_ref[...] + (acc if scale == 1.0 else scale * acc)
        x_ref[...] = x
        ms = jnp.mean(x * x, axis=-1, keepdims=True)
        h_ref[...] = ((x * lax.rsqrt(ms + EPS)) * g_ref[...]).astype(h_ref.dtype)


def matmul_residual_norm(a, w, layer, res, gains, gain_layer, scale=1.0):
    m, k = a.shape
    n = res.shape[1]
    n_pro = k // _ROWS_K_CHUNK
    fixed = k * n * 2 + 2 * _ROWS_K_CHUNK * n * 4
    for tm in (512, 256, 128):
        per_tile = 2 * tm * k * 2 + 2 * tm * n * (4 + 4 + 2) + tm * n * 4
        if m % tm == 0 and fixed + per_tile <= MATMUL_VMEM_BUDGET:
            break
    else:
        raise ValueError("no row tile fits VMEM")
    row = lambda s: (jnp.maximum(s - n_pro, 0), 0)
    return pl.pallas_call(
        functools.partial(_mm_rows_kernel, n_pro=n_pro, scale=scale),
        grid=(n_pro + m // tm,),
        in_specs=[pl.BlockSpec((tm, k), row),
                  pl.BlockSpec((None, _ROWS_K_CHUNK, n), lambda s: (layer, jnp.minimum(s, n_pro - 1), 0)),
                  pl.BlockSpec((tm, n), row),
                  pl.BlockSpec((None, 1, n), lambda s: (gain_layer, 0, 0))],
        out_specs=[pl.BlockSpec((tm, n), row), pl.BlockSpec((tm, n), row)],
        out_shape=[jax.ShapeDtypeStruct((m, n), F32), jax.ShapeDtypeStruct((m, n), BF16)],
        scratch_shapes=[pltpu.VMEM((k, n), BF16)],
        compiler_params=_params(1),
        name="matmul_residual_norm",
    )(a, w, res, gains)


S5_LANE_CHUNKS = S5_MODES // LANES
S5_POWERS = 9
S5_WIDTH = S5_GROUPS * S5_GROUP_CH
S5_SUPER = 4
S5_SUPER_CH = S5_WIDTH // S5_SUPER
S5_SUPER_MODES = S5_MODES // S5_SUPER
S5_SUPER_CHUNKS = S5_SUPER_MODES // LANES
S5_BLOCK_ROWS = 512
S5_TILE_TABLE = 4 * SUBLANES


def _cmul_add(xr, xi, ar, ai, br, bi, keep=None):
    pr = ar * br - ai * bi
    pi = ar * bi + ai * br
    if keep is not None:
        pr = jnp.where(keep, pr, 0.0)
        pi = jnp.where(keep, pi, 0.0)
    return xr + pr, xi + pi


def _s5_kernel(u_ref, h0r_ref, h0i_ref, pwr_ref, pwi_ref, lnr_ref, lni_ref, bsup_ref, csup_ref, d_ref, gw_ref,
               gb_ref, o_ref, hor_ref, hoi_ref, bur_ref, bui_ref, hs_ref, *, rb, chain):
    ntile = rb // SUBLANES
    u = u_ref[...]
    ub = u.astype(BF16)
    for q in range(S5_SUPER):
        bu = jnp.dot(ub[:, q * S5_SUPER_CH:(q + 1) * S5_SUPER_CH], bsup_ref[q], preferred_element_type=F32)
        for r in range(S5_SUPER_CHUNKS):
            bur_ref[q * S5_SUPER_CHUNKS + r] = bu[:, r * LANES:(r + 1) * LANES]
            bui_ref[q * S5_SUPER_CHUNKS + r] = bu[:, S5_SUPER_MODES + r * LANES:S5_SUPER_MODES + (r + 1) * LANES]

    if chain:
        @pl.when(pl.program_id(1) == 0)
        def _():
            hor_ref[...] = h0r_ref[...]
            hoi_ref[...] = h0i_ref[...]

    tile_idx = lax.broadcasted_iota(jnp.int32, (ntile, LANES), 0)
    tile_last = pl.ds(SUBLANES - 1, ntile, stride=SUBLANES)
    tile_steps = SUBLANES.bit_length() - 1

    def chunk(c, carry):
        pr = pwr_ref[c]
        pi = pwi_ref[c]
        xr = bur_ref[c].reshape(ntile, SUBLANES, LANES)
        xi = bui_ref[c].reshape(ntile, SUBLANES, LANES)
        for k in range(tile_steps):
            sh = 1 << k
            rows = slice((k + 1) * SUBLANES, (k + 2) * SUBLANES)
            xr, xi = _cmul_add(xr, xi, lnr_ref[c, rows, :], lni_ref[c, rows, :],
                               pltpu.roll(xr, sh, 1), pltpu.roll(xi, sh, 1))
        p8r, p8i = pr[tile_steps:tile_steps + 1], pi[tile_steps:tile_steps + 1]
        if chain:
            bur_ref[c] = xr.reshape(rb, LANES)
            bui_ref[c] = xi.reshape(rb, LANES)
            tr = bur_ref[c, tile_last, :]
            ti = bui_ref[c, tile_last, :]
            hr, hi = hor_ref[c], hoi_ref[c]
            first = tile_idx == 0
            cr, ci = _cmul_add(tr, ti, p8r, p8i, hr, hi, first)
            for j in range(ntile.bit_length() - 1):
                sh = 1 << j
                k = tile_steps + j
                cr, ci = _cmul_add(cr, ci, pr[k:k + 1], pi[k:k + 1], pltpu.roll(cr, sh, 0), pltpu.roll(ci, sh, 0),
                                   tile_idx >= sh)
            prev_r = jnp.where(first, hr, pltpu.roll(cr, 1, 0))
            prev_i = jnp.where(first, hi, pltpu.roll(ci, 1, 0))
            hor_ref[c] = cr[ntile - 1:ntile]
            hoi_ref[c] = ci[ntile - 1:ntile]
        else:
            prev_r, prev_i = h0r_ref[c], h0i_ref[c]
        lr = lnr_ref[c, 0:SUBLANES, :]
        li = lni_ref[c, 0:SUBLANES, :]
        for k in range(ntile):
            rows = slice(k * SUBLANES, (k + 1) * SUBLANES)
            yr, yi = _cmul_add(xr[k], xi[k], lr, li, prev_r[k:k + 1], prev_i[k:k + 1])
            bur_ref[c, rows, :] = yr
            bui_ref[c, rows, :] = yi
        if not chain:
            hor_ref[c] = bur_ref[c, tile_last, :]
            hoi_ref[c] = bui_ref[c, tile_last, :]
        col = pl.multiple_of((c // S5_SUPER_CHUNKS) * 2 * S5_SUPER_MODES + (c % S5_SUPER_CHUNKS) * LANES, LANES)
        hs_ref[:, pl.ds(col, LANES)] = bur_ref[c].astype(BF16)
        hs_ref[:, pl.ds(col + S5_SUPER_MODES, LANES)] = bui_ref[c].astype(BF16)
        return carry

    lax.fori_loop(0, S5_LANE_CHUNKS, chunk, 0)

    y = jnp.concatenate(
        [jnp.dot(hs_ref[:, q * 2 * S5_SUPER_MODES:(q + 1) * 2 * S5_SUPER_MODES], csup_ref[q],
                 preferred_element_type=F32) for q in range(S5_SUPER)], axis=1) + d_ref[...] * u
    y = jax.nn.gelu(y)
    gate = jax.nn.sigmoid(_bdot(y, gw_ref[...]) + gb_ref[...])
    o_ref[...] = (y * gate).astype(o_ref.dtype)


def s5_mixer(proj, mix, row0, n_seq, seq_len, h0r, h0i, tabs, layer):
    chain = seq_len > SUBLANES
    rb = S5_BLOCK_ROWS
    if chain:
        grid = (n_seq, seq_len // rb)
        rows = lambda s, b: (row0 // rb + s * (seq_len // rb) + b, 0)
        h0r = h0r.reshape(S5_LANE_CHUNKS, n_seq, 1, LANES)
        h0i = h0i.reshape(S5_LANE_CHUNKS, n_seq, 1, LANES)
        st_spec = pl.BlockSpec((S5_LANE_CHUNKS, None, 1, LANES), lambda s, b: (0, s, 0, 0))
        st_shape = jax.ShapeDtypeStruct((S5_LANE_CHUNKS, n_seq, 1, LANES), F32)
    else:
        assert seq_len == SUBLANES
        nseg = rb // seq_len
        grid = (n_seq // nseg, 1)
        rows = lambda s, b: (row0 // rb + s, 0)
        st_spec = pl.BlockSpec((S5_LANE_CHUNKS, nseg, LANES), lambda s, b: (0, s, 0))
        st_shape = jax.ShapeDtypeStruct((S5_LANE_CHUNKS, n_seq, LANES), F32)
    const3 = lambda s, b: (layer, 0, 0)
    const4 = lambda s, b: (layer, 0, 0, 0)
    pw_spec = pl.BlockSpec((None, S5_LANE_CHUNKS, S5_POWERS, LANES), const4)
    lin_spec = pl.BlockSpec((None, S5_LANE_CHUNKS, S5_TILE_TABLE, LANES), const4)
    mix, hr, hi = _pallas(
        functools.partial(_s5_kernel, rb=rb, chain=chain),
        grid=grid,
        in_specs=[pl.BlockSpec((rb, S5_WIDTH), rows), st_spec, st_spec, pw_spec, pw_spec, lin_spec, lin_spec,
                  pl.BlockSpec((None, S5_SUPER, S5_SUPER_CH, 2 * S5_SUPER_MODES), const4),
                  pl.BlockSpec((None, S5_SUPER, 2 * S5_SUPER_MODES, S5_SUPER_CH), const4),
                  pl.BlockSpec((None, 1, S5_WIDTH), const3),
                  pl.BlockSpec((None, S5_WIDTH, S5_WIDTH), const3),
                  pl.BlockSpec((None, 1, S5_WIDTH), const3)],
        args=[proj, h0r, h0i, tabs["pw_re"], tabs["pw_im"], tabs["lin_re"], tabs["lin_im"], tabs["bsup"],
              tabs["csup"], tabs["d"], tabs["glu_w"], tabs["glu_b"]],
        out_specs=[pl.BlockSpec((rb, S5_WIDTH), rows), st_spec, st_spec],
        out_shapes=[jax.ShapeDtypeStruct((proj.shape[0], MIX_WIDTH), BF16), st_shape, st_shape],
        carried=[mix],
        scratch=[pltpu.VMEM((S5_LANE_CHUNKS, rb, LANES), F32),
                 pltpu.VMEM((S5_LANE_CHUNKS, rb, LANES), F32),
                 pltpu.VMEM((rb, 2 * S5_MODES), BF16)],
        name="s5_mixer",
    )
    return mix, hr.reshape(S5_LANE_CHUNKS, n_seq, LANES), hi.reshape(S5_LANE_CHUNKS, n_seq, LANES)


def s5_tables(lam_re, lam_im, b_re, b_im, c_re, c_im, d, log_step, glu_w, glu_b):
    depth = lam_re.shape[0]
    step = jnp.exp(log_step.astype(F32))[..., None]
    lr, li = lam_re.astype(F32), lam_im.astype(F32)
    mag = jnp.exp(lr * step)
    lbr, lbi = mag * jnp.cos(li * step), mag * jnp.sin(li * step)
    den = lr * lr + li * li
    fr = ((lbr - 1.0) * lr + lbi * li) / den
    fi = (lbi * lr - (lbr - 1.0) * li) / den
    bbr = fr[..., None] * b_re - fi[..., None] * b_im
    bbi = fr[..., None] * b_im + fi[..., None] * b_re
    per = S5_GROUPS // S5_SUPER
    eye = jnp.eye(per, dtype=F32)

    def blk_in(t):
        t = t.reshape(depth, S5_SUPER, per, S5_STATE, S5_GROUP_CH)
        return jnp.einsum("dqgpc,gh->dqgchp", t, eye).reshape(depth, S5_SUPER, S5_SUPER_CH, S5_SUPER_MODES)

    def blk_out(t):
        t = t.reshape(depth, S5_SUPER, per, S5_GROUP_CH, S5_STATE)
        return jnp.einsum("dqgcp,gh->dqgphc", t, eye).reshape(depth, S5_SUPER, S5_SUPER_MODES, S5_SUPER_CH)

    bsup = jnp.concatenate([blk_in(bbr), blk_in(bbi)], axis=-1).astype(BF16)
    csup = jnp.concatenate([blk_out(c_re.astype(F32)), blk_out(-c_im.astype(F32))], axis=2).astype(BF16)
    pr, pi = lbr.reshape(depth, S5_MODES), lbi.reshape(depth, S5_MODES)
    prs, pis = [], []
    for _ in range(S5_POWERS):
        prs.append(pr)
        pis.append(pi)
        pr, pi = pr * pr - pi * pi, 2.0 * pr * pi
    lr, li = prs[0], pis[0]
    lrs, lis = [], []
    for _ in range(SUBLANES):
        lrs.append(lr)
        lis.append(li)
        lr, li = lr * prs[0] - li * pis[0], lr * pis[0] + li * prs[0]
    for k in range(SUBLANES.bit_length() - 1):
        for t in range(SUBLANES):
            lrs.append(prs[k] if t >= (1 << k) else jnp.zeros_like(prs[k]))
            lis.append(pis[k] if t >= (1 << k) else jnp.zeros_like(pis[k]))

    def chunked(ts):
        t = jnp.stack(ts, axis=1).reshape(depth, len(ts), S5_LANE_CHUNKS, LANES)
        return t.transpose(0, 2, 1, 3)

    return dict(pw_re=chunked(prs), pw_im=chunked(pis), lin_re=chunked(lrs), lin_im=chunked(lis),
                bsup=bsup, csup=csup, d=d.astype(F32).reshape(depth, 1, S5_WIDTH), glu_w=glu_w.astype(BF16),
                glu_b=glu_b.astype(F32).reshape(depth, 1, S5_WIDTH))


def _to_chunk_major(h):
    n = h.shape[0]
    return h.reshape(n, S5_LANE_CHUNKS, LANES).transpose(1, 0, 2)


def _from_chunk_major(h):
    n = h.shape[1]
    return h.transpose(1, 0, 2).reshape(n, S5_GROUPS, S5_STATE)


_RET_LOG_GAMMA = tuple(math.log(1.0 - 2.0 ** (-5.0 - h)) for h in range(RET_HEADS))
RET_WIDTH = RET_HEADS * RET_HEAD_DIM


def _ret_kernel(q_ref, k_ref, v_ref, g_ref, cos_ref, sin_ref, nw_ref, s0_ref, o_ref, s_ref, y_ref, *, t, nb):
    @pl.when(pl.program_id(1) == 0)
    def _():
        s_ref[...] = s0_ref[...]

    cos = cos_ref[...]
    sin = sin_ref[...]
    ii = lax.broadcasted_iota(jnp.int32, (t, t), 0)
    jj = lax.broadcasted_iota(jnp.int32, (t, t), 1)
    ti = lax.broadcasted_iota(jnp.int32, (t, 1), 0).astype(F32)
    half = RET_HEAD_DIM // 2
    for h in range(RET_HEADS):
        lg = _RET_LOG_GAMMA[h]
        cols = slice(h * RET_HEAD_DIM, (h + 1) * RET_HEAD_DIM)
        decay = jnp.exp(jnp.where(ii >= jj, (ii - jj).astype(F32) * lg, -jnp.inf))
        grow = jnp.exp((ti + 1.0) * lg)
        tail = jnp.exp((t - 1.0 - ti) * lg)
        for n in range(nb):
            rows = slice(n * t, (n + 1) * t)
            q = q_ref[rows, cols]
            k = k_ref[rows, cols]
            v = v_ref[rows, cols].astype(BF16)
            q = q * cos + pltpu.roll(q, half, 1) * sin
            k = (k * cos + pltpu.roll(k, half, 1) * sin) * (RET_HEAD_DIM ** -0.5)
            s_prev = s_ref[n, h]
            scores = lax.dot_general(q.astype(BF16), k.astype(BF16), _NT, preferred_element_type=F32) * decay
            y = _bdot(scores, v) + _bdot(q * grow, s_prev)
            s_ref[n, h] = math.exp(t * lg) * s_prev + lax.dot_general(
                (k * tail).astype(BF16), v, _TN, preferred_element_type=F32)
            y = y * lax.rsqrt(jnp.mean(y * y, axis=-1, keepdims=True) + EPS)
            y = y * nw_ref[:, cols]
            y_ref[rows, cols] = jax.nn.silu(g_ref[rows, cols]) * y
    o_ref[...] = y_ref[...].astype(o_ref.dtype)


def retention_mixer(proj, mix, row0, n_seq, seq_len, s0, s0_layer, s_out, depth, cos2, sin2, norm_w, layer):
    t = math.gcd(seq_len, CHUNK)
    n_chunks = seq_len // t
    nb = 1 if n_chunks > 1 else 8
    rb = nb * t

    def col(cb):
        return pl.BlockSpec((rb, RET_WIDTH), lambda s, c: (row0 // rb + s * n_chunks + c, cb))

    st_block = (None, nb, RET_HEADS, RET_HEAD_DIM, RET_HEAD_DIM)
    tab_spec = pl.BlockSpec((t, RET_HEAD_DIM), lambda s, c: (c, 0))
    return _pallas(
        functools.partial(_ret_kernel, t=t, nb=nb),
        grid=(n_seq // nb, n_chunks),
        in_specs=[col(1), col(2), col(3), col(4), tab_spec, tab_spec,
                  pl.BlockSpec((None, 1, RET_WIDTH), lambda s, c: (layer, 0, 0)),
                  pl.BlockSpec(st_block, lambda s, c: (s0_layer, s, 0, 0, 0))],
        args=[proj, proj, proj, proj, cos2, sin2, norm_w, s0],
        out_specs=[col(1), pl.BlockSpec(st_block, lambda s, c: (layer, s, 0, 0, 0))],
        out_shapes=[jax.ShapeDtypeStruct((proj.shape[0], MIX_WIDTH), BF16),
                    jax.ShapeDtypeStruct((depth,) + s0.shape[1:], F32)],
        carried=[mix, s_out],
        scratch=[pltpu.VMEM((rb, RET_WIDTH), F32)],
        name="retention_mixer",
    )


def rope_tables(pos):
    half = RET_HEAD_DIM // 2
    inv = ROPE_BASE ** (-jnp.arange(half, dtype=F32) / half)
    ang = pos.astype(F32)[:, None] * inv[None, :]
    cos, sin = jnp.cos(ang), jnp.sin(ang)
    return jnp.concatenate([cos, cos], axis=-1), jnp.concatenate([-sin, sin], axis=-1)


SSD_WIDTH = SSD_HEADS * SSD_HEAD_DIM
SSD_BC = SSD_GROUPS * SSD_STATE
SSD_CONV_DIM = SSD_WIDTH + 2 * SSD_BC
_SSD_PREV = SUBLANES
_COL_BLOCK = 512
_SSD_Z_COL = 2560
_SSD_XBC_COL = _SSD_Z_COL + SSD_WIDTH


def _ssd_kernel(z0_ref, z1_ref, x0_ref, x1_ref, x2_ref, dt_ref, cs_ref, cw_ref, cb_ref, dtb_ref, aneg_ref, dsk_ref,
                nw_ref, s0_ref, o_ref, s_ref, co_ref, ext_ref, xbc_ref, y_ref, xdt_ref, xtl_ref, fs_ref, *,
                t, nb, conv_rows):
    keep = SSD_CONV - 1
    rep = SSD_HEADS // SSD_GROUPS
    grp = rep * SSD_HEAD_DIM

    def conv_row(n):
        return pl.ds(n, 1) if conv_rows == nb else pl.ds(pl.program_id(0) * nb + n, 1)

    @pl.when(pl.program_id(1) == 0)
    def _():
        s_ref[...] = s0_ref[...]
        for n in range(nb):
            for r in range(keep):
                ext_ref[n, _SSD_PREV - keep + r:_SSD_PREV - keep + r + 1, :] = cs_ref[r, conv_row(n), :]

    rb = nb * t
    ii = lax.broadcasted_iota(jnp.int32, (rb, rb), 0)
    jj = lax.broadcasted_iota(jnp.int32, (rb, rb), 1)
    same_seq = (ii // t) == (jj // t)
    causal = jnp.logical_and(same_seq, ii >= jj)
    tri = causal.astype(BF16)
    seq_ones = same_seq.astype(BF16)
    cw = cw_ref[...]
    for n in range(nb):
        rows = slice(n * t, (n + 1) * t)
        ext_ref[n, _SSD_PREV:_SSD_PREV + t, 0:_COL_BLOCK] = x0_ref[rows, :]
        ext_ref[n, _SSD_PREV:_SSD_PREV + t, _COL_BLOCK:2 * _COL_BLOCK] = x1_ref[rows, :]
        ext_ref[n, _SSD_PREV:_SSD_PREV + t, 2 * _COL_BLOCK:] = x2_ref[rows, :]
        conv = cb_ref[...] + cw[keep:keep + 1] * ext_ref[n, _SSD_PREV:_SSD_PREV + t, :]
        for back in range(1, SSD_CONV):
            conv = conv + cw[keep - back:keep - back + 1] * ext_ref[n, pl.ds(_SSD_PREV - back, t), :]
        tail_rows = ext_ref[n, _SSD_PREV + t - keep:_SSD_PREV + t, :]
        for r in range(keep):
            co_ref[r, conv_row(n), :] = tail_rows[r:r + 1]
        ext_ref[n, _SSD_PREV - keep:_SSD_PREV, :] = tail_rows
        xbc_ref[rows, :] = jax.nn.silu(conv)

    dt = jax.nn.softplus(dt_ref[...] + dtb_ref[...])
    la = dt * aneg_ref[...]
    sums = jnp.dot(jnp.concatenate([jnp.concatenate([tri] * 3, axis=1), jnp.concatenate([seq_ones] * 3, axis=1)],
                                   axis=0),
                   jnp.concatenate(_split3(la), axis=0), preferred_element_type=F32)
    cum, tot = sums[:rb], sums[rb:]
    cum_t = cum.T
    e_last = jnp.exp(tot)

    spread = (lax.broadcasted_iota(jnp.int32, (3 * LANES, SSD_WIDTH), 0) % LANES
              == lax.broadcasted_iota(jnp.int32, (3 * LANES, SSD_WIDTH), 1) // SSD_HEAD_DIM).astype(BF16)
    factors = jnp.concatenate([dt, dt * jnp.exp(tot - cum), jnp.exp(cum)], axis=0)
    cols = jnp.dot(jnp.concatenate(_split3(factors), axis=1), spread, preferred_element_type=F32)
    xh = xbc_ref[:, :SSD_WIDTH]
    xdt_ref[...] = xh * cols[:rb]
    xtl_ref[...] = xh * cols[rb:2 * rb]
    e_cum_cols = cols[2 * rb:]
    first_of_pair = lax.broadcasted_iota(jnp.int32, (1, 2 * SSD_HEAD_DIM), 1) < SSD_HEAD_DIM
    for g in range(SSD_GROUPS):
        gc = slice(g * grp, (g + 1) * grp)
        bg = xbc_ref[:, SSD_WIDTH + g * SSD_STATE:SSD_WIDTH + (g + 1) * SSD_STATE]
        cg = xbc_ref[:, SSD_WIDTH + SSD_BC + g * SSD_STATE:SSD_WIDTH + SSD_BC + (g + 1) * SSD_STATE]
        gram = lax.dot_general(cg.astype(BF16), bg.astype(BF16), _NT, preferred_element_type=F32)
        for n in range(nb):
            rows = slice(n * t, (n + 1) * t)
            s_grp = s_ref[n, g * rep:(g + 1) * rep].reshape(grp, SSD_STATE)
            fs_ref[rows, gc] = lax.dot_general(cg[rows].astype(BF16), s_grp.astype(BF16), _NT,
                                               preferred_element_type=F32)
            s_add = lax.dot_general(xtl_ref[rows, gc].astype(BF16), bg[rows].astype(BF16), _TN,
                                    preferred_element_type=F32)
            for hh in range(rep):
                h = g * rep + hh
                s_ref[n, h] = (e_last[n * t:n * t + 1, h:h + 1] * s_ref[n, h]
                               + s_add[hh * SSD_HEAD_DIM:(hh + 1) * SSD_HEAD_DIM, :])
        for pair in range(rep // 2):
            h = g * rep + 2 * pair
            pc = slice(h * SSD_HEAD_DIM, (h + 2) * SSD_HEAD_DIM)
            xp = xdt_ref[:, pc]
            scores = [gram * jnp.exp(jnp.where(causal, cum[:, k:k + 1] - cum_t[k:k + 1, :], -jnp.inf))
                      for k in (h, h + 1)]
            y_ref[:, pc] = _bdot(jnp.concatenate(scores, axis=1),
                                 jnp.concatenate([jnp.where(first_of_pair, xp, 0.0),
                                                  jnp.where(first_of_pair, 0.0, xp)], axis=0))
    y = y_ref[...] + e_cum_cols * fs_ref[...] + xh * dsk_ref[...]
    y = jnp.concatenate([y[:, :_COL_BLOCK] * jax.nn.silu(z0_ref[...]),
                         y[:, _COL_BLOCK:] * jax.nn.silu(z1_ref[...])], axis=1)
    y = y * lax.rsqrt(jnp.mean(y * y, axis=-1, keepdims=True) + EPS)
    o_ref[...] = (y * nw_ref[...]).astype(o_ref.dtype)


def ssd_mixer(proj, dt_raw, mix, row0, n_seq, seq_len, s0, conv0, st_layer, s_out, conv_out, depth, tabs, layer):
    t = math.gcd(seq_len, CHUNK)
    n_chunks = seq_len // t
    nb = 1 if n_chunks > 1 else SUBLANES
    rb = nb * t
    conv_rows = nb if nb % SUBLANES == 0 else n_seq

    def col(cb, width=_COL_BLOCK):
        return pl.BlockSpec((rb, width), lambda s, c: (row0 // rb + s * n_chunks + c, cb))

    def conv_idx(lyr):
        return lambda s, c: (lyr, 0, s if conv_rows == nb else 0, 0)

    z_cb = _SSD_Z_COL // _COL_BLOCK
    x_cb = _SSD_XBC_COL // _COL_BLOCK
    const3 = lambda s, c: (layer, 0, 0)
    st_block = (None, nb, SSD_HEADS, SSD_HEAD_DIM, SSD_STATE)
    cs_block = (None, SSD_CONV - 1, conv_rows, SSD_CONV_DIM)
    return _pallas(
        functools.partial(_ssd_kernel, t=t, nb=nb, conv_rows=conv_rows),
        grid=(n_seq // nb, n_chunks),
        in_specs=[col(z_cb), col(z_cb + 1), col(x_cb), col(x_cb + 1), col(x_cb + 2), col(0, LANES),
                  pl.BlockSpec(cs_block, conv_idx(st_layer)),
                  pl.BlockSpec((None, SSD_CONV, SSD_CONV_DIM), const3),
                  pl.BlockSpec((None, 1, SSD_CONV_DIM), const3),
                  pl.BlockSpec((None, 1, LANES), const3),
                  pl.BlockSpec((None, 1, LANES), const3),
                  pl.BlockSpec((None, 1, SSD_WIDTH), const3),
                  pl.BlockSpec((None, 1, SSD_WIDTH), const3),
                  pl.BlockSpec(st_block, lambda s, c: (st_layer, s, 0, 0, 0))],
        args=[proj, proj, proj, proj, proj, dt_raw, conv0, tabs["conv_w"], tabs["conv_b"], tabs["dt_bias"],
              tabs["a_neg"], tabs["d"], tabs["norm"], s0],
        out_specs=[col(1, SSD_WIDTH),
                   pl.BlockSpec(st_block, lambda s, c: (layer, s, 0, 0, 0)),
                   pl.BlockSpec(cs_block, conv_idx(layer))],
        out_shapes=[jax.ShapeDtypeStruct((proj.shape[0], MIX_WIDTH), BF16),
                    jax.ShapeDtypeStruct((depth,) + s0.shape[1:], F32),
                    jax.ShapeDtypeStruct((depth,) + conv0.shape[1:], F32)],
        carried=[mix, s_out, conv_out],
        scratch=[pltpu.VMEM((nb, _SSD_PREV + t, SSD_CONV_DIM), F32),
                 pltpu.VMEM((rb, SSD_CONV_DIM), F32)] + [pltpu.VMEM((rb, SSD_WIDTH), F32)] * 4,
        name="ssd_mixer",
    )


def _softmax(s):
    s = s - jnp.max(s, axis=-1, keepdims=True)
    p = jnp.exp(s)
    return p / jnp.sum(p, axis=-1, keepdims=True)


def _xattn_kernel(q_ref, k_ref, v_ref, o_ref, *, tq, nb, head_dim, heads_axis):
    if not heads_axis:
        for h in range(XATTN_HEADS):
            cols = slice(h * head_dim, (h + 1) * head_dim)
            q = q_ref[:, cols]
            pieces = []
            for n in range(nb):
                s = lax.dot_general(q, k_ref[n, :, cols].astype(BF16), _NT,
                                    preferred_element_type=F32)[n * tq:(n + 1) * tq]
                pieces.append(_bdot(_softmax(s * (head_dim ** -0.5)), v_ref[n, :, cols]))
            att = pieces[0] if nb == 1 else jnp.concatenate(pieces, axis=0)
            o_ref[:, cols] = att.astype(o_ref.dtype)
        return

    flat = MEM_TOKENS * XATTN_HEADS
    rows = XATTN_HEADS * tq
    row_head = lax.broadcasted_iota(jnp.int32, (rows, flat), 0) // tq
    col_head = lax.broadcasted_iota(jnp.int32, (rows, flat), 1) % XATTN_HEADS
    same_head = row_head == col_head
    q_all = q_ref[...].astype(F32)
    outs = []
    for n in range(nb):
        q = jnp.concatenate([q_all[n * tq:(n + 1) * tq, h * head_dim:(h + 1) * head_dim]
                             for h in range(XATTN_HEADS)], axis=0)
        k = k_ref[n].reshape(flat, head_dim)
        v = v_ref[n].reshape(flat, head_dim)
        s = lax.dot_general(q.astype(BF16), k.astype(BF16), _NT, preferred_element_type=F32)
        s = jnp.where(same_head, s * (head_dim ** -0.5), -jnp.inf)
        outs.append(_bdot(_softmax(s), v))
    for h in range(XATTN_HEADS):
        att = jnp.concatenate([o[h * tq:(h + 1) * tq] for o in outs], axis=0)
        o_ref[:, h * head_dim:(h + 1) * head_dim] = att.astype(o_ref.dtype)


def cross_attention(q, att, row0, n_seq, seq_len, mem_k, mem_v, layer=None):
    d = q.shape[-1]
    head_dim = d // XATTN_HEADS
    if seq_len > SUBLANES:
        tq, nb = 512, 1
    else:
        tq, nb = seq_len, 4
    n_q = seq_len // tq
    rb = nb * tq
    if layer is None:
        kv_spec = pl.BlockSpec((nb, MEM_TOKENS, d), lambda s, i: (s, 0, 0))
    else:
        kv_spec = pl.BlockSpec((None, nb, MEM_TOKENS, XATTN_HEADS, head_dim), lambda s, i: (layer, s, 0, 0, 0))
    q_spec = pl.BlockSpec((rb, d), lambda s, i: (row0 // rb + s * n_q + i, 0))
    (att,) = _pallas(
        functools.partial(_xattn_kernel, tq=tq, nb=nb, head_dim=head_dim, heads_axis=layer is not None),
        grid=(n_seq // nb, n_q),
        in_specs=[q_spec, kv_spec, kv_spec],
        args=[q, mem_k, mem_v],
        out_specs=[q_spec],
        out_shapes=[jax.ShapeDtypeStruct(q.shape, BF16)],
        carried=[att],
        name="cross_attention",
    )
    return att


def kernel(x_prompt, x_sample, mem_prompt, state_s5_re, state_s5_im, state_ret, state_ssm, state_conv, cache_mem_k, cache_mem_v, ffn1_norm, ffn1_w1, ffn1_w3, ffn1_w2, mix_norm, w_in, w_out, s5_lambda_re, s5_lambda_im, s5_b_re, s5_b_im, s5_c_re, s5_c_im, s5_d, s5_log_step, s5_glu_w, s5_glu_b, ret_norm, ssd_conv_w, ssd_conv_b, ssd_dt_bias, ssd_a_log, ssd_d, ssd_norm, xattn_norm, xattn_wq, xattn_wk, xattn_wv, xattn_wo, ffn2_norm, ffn2_w1, ffn2_w3, ffn2_w2, final_norm):
    bp, lp, d = x_prompt.shape
    bs, ls, _ = x_sample.shape
    depth = w_in.shape[0]
    ffn = ffn1_w1.shape[-1]
    n_p, n_s = bp * lp, bs * ls

    def gain(g):
        return g.astype(F32).reshape(-1, 1, g.shape[-1])

    def lane_pad(v):
        return jnp.pad(v.astype(F32), ((0, 0), (0, LANES - v.shape[-1]))).reshape(depth, 1, LANES)

    s5_tabs = s5_tables(s5_lambda_re, s5_lambda_im, s5_b_re, s5_b_im, s5_c_re, s5_c_im, s5_d, s5_log_step,
                        s5_glu_w, s5_glu_b)
    main_cols = w_in.shape[-1] - SSD_HEADS
    w_dt = jnp.pad(w_in[:, :, main_cols:], ((0, 0), (0, 0), (0, LANES - SSD_HEADS)))
    w_in_t = w_in.swapaxes(1, 2)
    ssd_tabs = dict(conv_w=ssd_conv_w.astype(F32), conv_b=gain(ssd_conv_b), dt_bias=lane_pad(ssd_dt_bias),
                    a_neg=lane_pad(-jnp.exp(ssd_a_log.astype(F32))),
                    d=gain(jnp.repeat(ssd_d, SSD_HEAD_DIM, axis=-1)), norm=gain(ssd_norm))
    ret_w = gain(ret_norm)
    cos_p, sin_p = rope_tables(jnp.arange(lp, dtype=jnp.int32))
    cos_s, sin_s = rope_tables(PAST_LEN + jnp.arange(ls, dtype=jnp.int32))
    mem = mem_prompt.reshape(bp * MEM_TOKENS, d).astype(BF16)
    norms = {name: gain(g) for name, g in dict(ffn1=ffn1_norm, mix=mix_norm, xattn=xattn_norm, ffn2=ffn2_norm,
                                                final=final_norm).items()}

    ssm_t = state_ssm.swapaxes(-1, -2)
    conv_t = state_conv.swapaxes(1, 2)
    s5_re_cm = jnp.stack([_to_chunk_major(state_s5_re[l]) for l in range(depth)])
    s5_im_cm = jnp.stack([_to_chunk_major(state_s5_im[l]) for l in range(depth)])
    zeros_s5 = jnp.zeros((S5_LANE_CHUNKS, bp, LANES), F32)
    zeros_ret = jnp.zeros((1, bp) + state_ret.shape[2:], F32)
    zeros_ssm = jnp.zeros((1, bp) + ssm_t.shape[2:], F32)
    zeros_conv = jnp.zeros((1, SSD_CONV - 1, bp, SSD_CONV_DIM), F32)

    x = jnp.concatenate([x_prompt.reshape(n_p, d), x_sample.reshape(n_s, d)], axis=0)
    lists = {name: [] for name in ("p_s5re", "p_s5im", "p_mk", "p_mv", "s_s5re", "s_s5im")}
    p_ret = jnp.zeros((depth, bp) + state_ret.shape[2:], F32)
    s_ret = jnp.zeros(state_ret.shape, F32)
    p_ssm = jnp.zeros((depth, bp) + ssm_t.shape[2:], F32)
    s_ssm = jnp.zeros(ssm_t.shape, F32)
    p_conv = jnp.zeros((depth, SSD_CONV - 1, bp, SSD_CONV_DIM), F32)
    s_conv = jnp.zeros(conv_t.shape, F32)

    h = rmsnorm(x, norms["ffn1"], 0, BF16)
    assert MIX_WIDTH == d
    spare = h
    for l in range(depth):
        a = matmul(h, [ffn1_w1, ffn1_w3], l, n_cols=ffn, out_dtype=BF16)
        x, h = matmul_residual_norm(a, ffn1_w2, l, x, norms["mix"], l, scale=0.5)

        proj = matmul(h, [w_in_t], l, n_cols=main_cols, out_dtype=F32, w_transposed=True)
        dt_raw = matmul(h, [w_dt], l, n_cols=LANES, out_dtype=F32)

        mix, hr_p, hi_p = s5_mixer(proj, spare, 0, bp, lp, zeros_s5, zeros_s5, s5_tabs, l)
        mix, hr_s, hi_s = s5_mixer(proj, mix, n_p, bs, ls, s5_re_cm[l], s5_im_cm[l], s5_tabs, l)
        mix, p_ret = retention_mixer(proj, mix, 0, bp, lp, zeros_ret, 0, p_ret, depth, cos_p, sin_p, ret_w, l)
        mix, s_ret = retention_mixer(proj, mix, n_p, bs, ls, state_ret, l, s_ret, depth, cos_s, sin_s, ret_w, l)
        mix, p_ssm, p_conv = ssd_mixer(proj, dt_raw, mix, 0, bp, lp, zeros_ssm, zeros_conv, 0, p_ssm, p_conv,
                                       depth, ssd_tabs, l)
        mix, s_ssm, s_conv = ssd_mixer(proj, dt_raw, mix, n_p, bs, ls, ssm_t, conv_t, l, s_ssm, s_conv,
                                       depth, ssd_tabs, l)
        x, h = matmul_residual_norm(mix, w_out, l, x, norms["xattn"], l)
        q = matmul(h, [xattn_wq], l, n_cols=d, out_dtype=BF16)
        mk = matmul(mem, [xattn_wk], l, n_cols=d, out_dtype=F32)
        mv = matmul(mem, [xattn_wv], l, n_cols=d, out_dtype=F32)
        att = cross_attention(q, mix, 0, bp, lp, mk.reshape(bp, MEM_TOKENS, d), mv.reshape(bp, MEM_TOKENS, d))
        att = cross_attention(q, att, n_p, bs, ls, cache_mem_k, cache_mem_v, layer=l)
        x, h = matmul_residual_norm(att, xattn_wo, l, x, norms["ffn2"], l)
        spare = att
        a = matmul(h, [ffn2_w1, ffn2_w3], l, n_cols=ffn, out_dtype=BF16)
        if l + 1 < depth:
            x, h = matmul_residual_norm(a, ffn2_w2, l, x, norms["ffn1"], l + 1, scale=0.5)
        else:
            x = matmul(a, [ffn2_w2], l, n_cols=d, out_dtype=F32, res=x, scale=0.5)

        lists["p_s5re"].append(_from_chunk_major(hr_p))
        lists["p_s5im"].append(_from_chunk_major(hi_p))
        lists["p_mk"].append(mk.reshape(bp, MEM_TOKENS, XATTN_HEADS, d // XATTN_HEADS))
        lists["p_mv"].append(mv.reshape(bp, MEM_TOKENS, XATTN_HEADS, d // XATTN_HEADS))
        lists["s_s5re"].append(_from_chunk_major(hr_s))
        lists["s_s5im"].append(_from_chunk_major(hi_s))

    y_p = rmsnorm(x, norms["final"], 0, F32, 0, n_p)
    y_s = rmsnorm(x, norms["final"], 0, F32, n_p, n_s)
    st = {name: jnp.stack(v) for name, v in lists.items()}
    return (y_p.reshape(bp, lp, d), y_s.reshape(bs, ls, d),
            st["p_s5re"], st["p_s5im"], p_ret, p_ssm.swapaxes(-1, -2), p_conv.swapaxes(1, 2),
            st["p_mk"], st["p_mv"],
            st["s_s5re"], st["s_s5im"], s_ret, s_ssm.swapaxes(-1, -2), s_conv.swapaxes(1, 2))
```
